```python
import jax
import jax.numpy as jnp
from jax import lax
import numpy as np

D_MODEL = 1024
BATCH = 2
SEQ = 8192
DEPTH = 2

GRID_W = 64
CTX_LEN = 256
N_MOD = 9
D_FF = 2816
GLA_HEADS = 4
GLA_DK = 64
GLA_DV = 128
GLA_GATE_RANK = 16
GLA_TAU = 16.0
GLA_CHUNK = 64
FOURIER_GROUPS = 4
FOURIER_CH = 128
MLA_HEADS = 8
MLA_NOPE = 64
MLA_ROPE = 32
MLA_V = 64
MLA_Q_RANK = 384
MLA_KV_RANK = 256
ROPE_BASE = 10000.0
Q_BLOCK = 128
N_BRANCH = 3
BRANCH_W = 512
EPS = 1e-6
IN_SPLIT = (
    ('gla_q', GLA_HEADS * GLA_DK),
    ('gla_k', GLA_HEADS * GLA_DK),
    ('gla_v', GLA_HEADS * GLA_DV),
    ('gla_g', GLA_HEADS * GLA_DV),
    ('gla_a_f', GLA_GATE_RANK),
    ('gla_a_b', GLA_GATE_RANK),
    ('fourier', FOURIER_GROUPS * FOURIER_CH),
    ('mla_cq', MLA_Q_RANK),
    ('mla_ckv', MLA_KV_RANK),
    ('mla_kr', MLA_ROPE),
    ('gates', N_BRANCH * D_MODEL),
)
D_IN = sum(s for _, s in IN_SPLIT)

kernel_name = 'hybrid_gla_fnet_mla_macaron'


def _rmsnorm(x, g):
    xf = x.astype(jnp.float32)
    y = xf * lax.rsqrt(jnp.mean(xf * xf, axis=-1, keepdims=True) + EPS)
    return (y * g.astype(jnp.float32)).astype(x.dtype)


def _modulation(cond, w, b):
    m = jax.nn.silu(cond) @ w + b
    return jnp.split(m[:, None, :], N_MOD, axis=-1)


def _half_ffn(x, mod, g_pre, g_post, wg, wu, wd):
    shift, scale, gate = mod
    h = _rmsnorm(x, g_pre) * (1 + scale) + shift
    y = (jax.nn.silu(h @ wg) * (h @ wu)) @ wd
    return x + 0.5 * gate * _rmsnorm(y, g_post)


def _split_in(z):
    out = {}
    off = 0
    for name, size in IN_SPLIT:
        out[name] = z[..., off:off + size]
        off += size
    return out


def _axial_rope_tables(n):
    ROWS = n // GRID_W
    rows = jnp.repeat(jnp.arange(ROWS, dtype=jnp.float32), GRID_W)
    cols = jnp.tile(jnp.arange(GRID_W, dtype=jnp.float32), ROWS)
    half = MLA_ROPE // 2
    inv_freq = ROPE_BASE ** (-jnp.arange(0, half, 2, dtype=jnp.float32) / half)
    ang_r = rows[:, None] * inv_freq
    ang_c = cols[:, None] * inv_freq
    return (jnp.cos(ang_r)[:, None], jnp.sin(ang_r)[:, None], jnp.cos(ang_c)[:, None], jnp.sin(ang_c)[:, None])


def _rotate(x, cos, sin):
    h = x.shape[-1] // 2
    x1, x2 = x[..., :h], x[..., h:]
    return jnp.concatenate([x1 * cos - x2 * sin, x2 * cos + x1 * sin], axis=-1)


def _rope2d(x, tabs):
    cr, sr, cc, sc = tabs
    xf = x.astype(jnp.float32)
    a = MLA_ROPE // 2
    out = jnp.concatenate([_rotate(xf[..., :a], cr, sr), _rotate(xf[..., a:], cc, sc)], axis=-1)
    return out.astype(x.dtype)


def _gla_chunked(q, k, v, log_a, s0):
    b, n, h, dk = q.shape
    dv = v.shape[-1]
    nc = n // GLA_CHUNK

    def chunks(t):
        return t.astype(jnp.float32).reshape(b, nc, GLA_CHUNK, h, t.shape[-1]).transpose(1, 0, 3, 2, 4)

    tril = jnp.tril(jnp.ones((GLA_CHUNK, GLA_CHUNK), bool))[:, :, None]

    def step(state, inp):
        qc, kc, vc, gc = inp
        cum = jnp.cumsum(gc, axis=2)
        rel = jnp.where(tril, cum[:, :, :, None, :] - cum[:, :, None, :, :], -jnp.inf)
        attn = jnp.einsum('bhtd,bhtsd,bhsd->bhts', qc, jnp.exp(rel), kc)
        out = jnp.einsum('bhts,bhse->bhte', attn, vc) + jnp.einsum('bhtd,bhde->bhte', qc * jnp.exp(cum), state)
        tot = cum[:, :, -1:, :]
        state = jnp.exp(tot)[:, :, 0, :, None] * state + jnp.einsum('bhsd,bhse->bhde', kc * jnp.exp(tot - cum), vc)
        return state, out

    s_fin, o = lax.scan(step, s0, (chunks(q), chunks(k), chunks(v), chunks(log_a)))
    return o.transpose(1, 0, 3, 2, 4).reshape(b, n, h, dv), s_fin


def _gla_branch(z, s0_f, s0_b, w_dec, b_dec, g_norm):
    b, n, _ = z['gla_q'].shape
    dt = z['gla_v'].dtype

    def heads(t, d):
        return t.reshape(b, n, GLA_HEADS, d)

    q = heads(z['gla_q'], GLA_DK) * GLA_DK ** -0.5
    k = heads(z['gla_k'], GLA_DK)
    v = heads(z['gla_v'], GLA_DV)
    la_f = heads(jax.nn.log_sigmoid((z['gla_a_f'] @ w_dec[0] + b_dec[0]).astype(jnp.float32)) / GLA_TAU, GLA_DK)
    la_b = heads(jax.nn.log_sigmoid((z['gla_a_b'] @ w_dec[1] + b_dec[1]).astype(jnp.float32)) / GLA_TAU, GLA_DK)
    o_f, s_f = _gla_chunked(q, k, v, la_f, s0_f)
    o_b, s_b = _gla_chunked(q[:, ::-1], k[:, ::-1], v[:, ::-1], la_b[:, ::-1], s0_b)
    o = (o_f + o_b[:, ::-1]).astype(dt)
    y = _rmsnorm(o, g_norm) * jax.nn.silu(heads(z['gla_g'], GLA_DV))
    return y.reshape(b, n, GLA_HEADS * GLA_DV), s_f, s_b


def _fourier(f):
    b, n, _ = f.shape
    g = f.astype(jnp.float32).reshape(b, n, FOURIER_GROUPS, FOURIER_CH)
    y = jnp.fft.fft2(g, axes=(1, 3), norm='ortho').real
    return y.reshape(b, n, FOURIER_GROUPS * FOURIER_CH).astype(f.dtype)


def _mla_project(z, q_norm, w_uq, kv_norm, w_ukv):
    b, n, _ = z['mla_cq'].shape
    q = (_rmsnorm(z['mla_cq'], q_norm) @ w_uq).reshape(b, n, MLA_HEADS, MLA_NOPE + MLA_ROPE)
    kv = (_rmsnorm(z['mla_ckv'], kv_norm) @ w_ukv).reshape(b, n, MLA_HEADS, MLA_NOPE + MLA_V)
    return q[..., :MLA_NOPE], q[..., MLA_NOPE:], kv[..., :MLA_NOPE], z['mla_kr'], kv[..., MLA_NOPE:]


def _mla_context_attention(qn, qr, kn, kr, v):
    scale = (MLA_NOPE + MLA_ROPE) ** -0.5
    s = jnp.einsum('bqhd,bkhd->bhqk', qn, kn) + jnp.einsum('bqhr,bkr->bhqk', qr, kr)
    p = jax.nn.softmax(s.astype(jnp.float32) * scale, axis=-1).astype(v.dtype)
    o = jnp.einsum('bhqk,bkhe->bqhe', p, v)
    return o.reshape(o.shape[0], o.shape[1], MLA_HEADS * MLA_V)


def _mla_latent_attention(qn, qr, kn, kr, v, kn_c, kr_c, v_c):
    b, n, h, _ = qn.shape
    nb = n // Q_BLOCK
    scale = (MLA_NOPE + MLA_ROPE) ** -0.5

    def blocks(t):
        return t.reshape(b, nb, Q_BLOCK, *t.shape[2:]).swapaxes(0, 1)

    def attend(blk):
        qn_i, qr_i = blk
        s_lat = jnp.einsum('bqhd,bkhd->bhqk', qn_i, kn) + jnp.einsum('bqhr,bkr->bhqk', qr_i, kr)
        s_ctx = jnp.einsum('bqhd,bkhd->bhqk', qn_i, kn_c) + jnp.einsum('bqhr,bkr->bhqk', qr_i, kr_c)
        s = jnp.concatenate([s_lat, s_ctx], axis=-1).astype(jnp.float32) * scale
        p = jax.nn.softmax(s, axis=-1).astype(v.dtype)
        return jnp.einsum('bhqk,bkhe->bqhe', p[..., :n], v) + jnp.einsum('bhqk,bkhe->bqhe', p[..., n:], v_c)

    o = lax.map(attend, (blocks(qn), blocks(qr)))
    return o.swapaxes(0, 1).reshape(b, n, h * MLA_V)


def _merge(ya, yb, yc, gates, w_branch, w_out):
    g = jax.nn.sigmoid(gates.astype(jnp.float32)).astype(ya.dtype)
    ga, gb, gc = jnp.split(g, N_BRANCH, axis=-1)
    m = ga * (ya @ w_branch[0]) + gb * (yb @ w_branch[1]) + gc * (yc @ w_branch[2])
    return m @ w_out


def _token_mixing(xl, xc, mod_l, mod_c, g_pre, g_post, w_in, w_dec, b_dec, g_gla, q_norm, w_uq, kv_norm, w_ukv, w_branch, w_out, rope, ctx_out):
    shift_l, scale_l, gate_l = mod_l
    shift_c, scale_c, gate_c = mod_c
    hl = _rmsnorm(xl, g_pre) * (1 + scale_l) + shift_l
    hc = _rmsnorm(xc, g_pre) * (1 + scale_c) + shift_c
    zl = _split_in(hl @ w_in)
    zc = _split_in(hc @ w_in)
    zero = jnp.zeros((xc.shape[0], GLA_HEADS, GLA_DK, GLA_DV), jnp.float32)
    ya_c, s_f, s_b = _gla_branch(zc, zero, zero, w_dec, b_dec, g_gla)
    ya_l, _, _ = _gla_branch(zl, s_f, s_b, w_dec, b_dec, g_gla)
    yb_l = _fourier(zl['fourier'])
    qn_c, qr_c, kn_c, kr_c, v_c = _mla_project(zc, q_norm, w_uq, kv_norm, w_ukv)
    qn_l, qr_l, kn_l, kr_l, v_l = _mla_project(zl, q_norm, w_uq, kv_norm, w_ukv)
    qr_l = _rope2d(qr_l, rope)
    kr_l = _rope2d(kr_l[:, :, None, :], rope)[:, :, 0, :]
    yc_l = _mla_latent_attention(qn_l, qr_l, kn_l, kr_l, v_l, kn_c, kr_c, v_c)
    xl = xl + gate_l * _rmsnorm(_merge(ya_l, yb_l, yc_l, zl['gates'], w_branch, w_out), g_post)
    if ctx_out:
        yb_c = _fourier(zc['fourier'])
        yc_c = _mla_context_attention(qn_c, qr_c, kn_c, kr_c, v_c)
        xc = xc + gate_c * _rmsnorm(_merge(ya_c, yb_c, yc_c, zc['gates'], w_branch, w_out), g_post)
    return xl, xc


def setup_inputs(seed: int = 0) -> dict:
    key = jax.random.key(seed)
    ks = jax.random.split(key, 21)
    L, D = DEPTH, D_MODEL

    def nrm(k, shape, scale=1.0):
        return scale * jax.random.normal(k, shape, jnp.float32)

    return {
        'x': nrm(ks[0], (BATCH, SEQ, D)),
        'c': nrm(ks[1], (BATCH, D)),
        'ctx': nrm(ks[2], (BATCH, CTX_LEN, D)),
        'c_ctx': nrm(ks[3], (D,)),
        'w_mod': nrm(ks[4], (L, D, N_MOD * D), 0.5 * D ** -0.5),
        'b_mod': nrm(ks[5], (L, N_MOD * D), 0.01),
        'norm_pre': 1.0 + nrm(ks[6], (L, 3, D), 0.02),
        'norm_post': 1.0 + nrm(ks[7], (L, 3, D), 0.02),
        'ffn_w_gate': nrm(ks[8], (L, 2, D, D_FF), D ** -0.5),
        'ffn_w_up': nrm(ks[9], (L, 2, D, D_FF), D ** -0.5),
        'ffn_w_down': nrm(ks[10], (L, 2, D_FF, D), D_FF ** -0.5),
        'w_in': nrm(ks[11], (L, D, D_IN), D ** -0.5),
        'gla_w_decay': nrm(ks[12], (L, 2, GLA_GATE_RANK, GLA_HEADS * GLA_DK), GLA_GATE_RANK ** -0.5),
        'gla_b_decay': nrm(ks[13], (L, 2, GLA_HEADS * GLA_DK), 0.01),
        'gla_norm': 1.0 + nrm(ks[14], (L, GLA_DV), 0.02),
        'mla_q_norm': 1.0 + nrm(ks[15], (L, MLA_Q_RANK), 0.02),
        'mla_w_uq': nrm(ks[16], (L, MLA_Q_RANK, MLA_HEADS * (MLA_NOPE + MLA_ROPE)), MLA_Q_RANK ** -0.5),
        'mla_kv_norm': 1.0 + nrm(ks[17], (L, MLA_KV_RANK), 0.02),
        'mla_w_ukv': nrm(ks[18], (L, MLA_KV_RANK, MLA_HEADS * (MLA_NOPE + MLA_V)), MLA_KV_RANK ** -0.5),
        'w_branch': nrm(ks[19], (L, N_BRANCH, BRANCH_W, D), BRANCH_W ** -0.5),
        'w_out': nrm(ks[20], (L, D, D), D ** -0.5),
    }


def reference(x, c, ctx, c_ctx, w_mod, b_mod, norm_pre, norm_post, ffn_w_gate, ffn_w_up, ffn_w_down, w_in, gla_w_decay, gla_b_decay, gla_norm, mla_q_norm, mla_w_uq, mla_kv_norm, mla_w_ukv, w_branch, w_out):
    rope = _axial_rope_tables(x.shape[1])
    xl, xc = x, ctx
    for layer in range(DEPTH):
        last = layer == DEPTH - 1
        mod_l = _modulation(c, w_mod[layer], b_mod[layer])
        mod_c = _modulation(c_ctx[None, :], w_mod[layer], b_mod[layer])
        ffn_a = (norm_pre[layer, 0], norm_post[layer, 0], ffn_w_gate[layer, 0], ffn_w_up[layer, 0], ffn_w_down[layer, 0])
        ffn_b = (norm_pre[layer, 2], norm_post[layer, 2], ffn_w_gate[layer, 1], ffn_w_up[layer, 1], ffn_w_down[layer, 1])
        xl = _half_ffn(xl, mod_l[0:3], *ffn_a)
        xc = _half_ffn(xc, mod_c[0:3], *ffn_a)
        xl, xc = _token_mixing(xl, xc, mod_l[3:6], mod_c[3:6], norm_pre[layer, 1], norm_post[layer, 1], w_in[layer], gla_w_decay[layer], gla_b_decay[layer], gla_norm[layer], mla_q_norm[layer], mla_w_uq[layer], mla_kv_norm[layer], mla_w_ukv[layer], w_branch[layer], w_out[layer], rope, not last)
        xl = _half_ffn(xl, mod_l[6:9], *ffn_b)
        if not last:
            xc = _half_ffn(xc, mod_c[6:9], *ffn_b)
    return xl
```

```python
import functools

import jax
import jax.numpy as jnp
from jax import lax
from jax.experimental import pallas as pl
from jax.experimental.pallas import tpu as pltpu

F32 = jnp.float32
BF16 = jnp.bfloat16

GRID_W = 64
N_MOD = 9
GLA_HEADS = 4
GLA_DK = 64
GLA_DV = 128
GLA_GATE_RANK = 16
GLA_TAU = 16.0
GLA_CHUNK = 64
GLA_EXP_CLAMP = 80.0
FOURIER_GROUPS = 4
FOURIER_CH = 128
FFT_NA = 64
MLA_HEADS = 8
MLA_NOPE = 64
MLA_ROPE = 32
MLA_V = 64
MLA_SLOT = 128
ROPE_BASE = 10000.0
EPS = 1e-6
LANES = 128
MOD_ROWS = 8
VMEM_LIMIT = 56 * 1024 * 1024

NT = (((1,), (1,)), ((), ()))
TN = (((0,), (0,)), ((), ()))


def _params(*sem):
    return pltpu.CompilerParams(dimension_semantics=sem, vmem_limit_bytes=VMEM_LIMIT)


def _const_spec(shape):
    nd = len(shape)
    return pl.BlockSpec(shape, lambda *_: (0,) * nd, pipeline_mode=pl.Buffered(1))


def _dot(a, b):
    return jnp.dot(a, b, preferred_element_type=F32)


def _rms(x, g):
    return x * lax.rsqrt(jnp.mean(x * x, axis=-1, keepdims=True) + EPS) * g


def _silu(x):
    return x * jax.nn.sigmoid(x)


def _modulated(x, g, mod_ref, sub):
    shift = mod_ref[3 * sub + 0:3 * sub + 1, :]
    scale = mod_ref[3 * sub + 1:3 * sub + 2, :]
    return _rms(x, g) * (1.0 + scale) + shift


def _mod_kernel(c_ref, w_ref, b_ref, o_ref):
    a = _silu(c_ref[...]).astype(BF16)
    o_ref[0] = _dot(a, w_ref[0].astype(BF16)) + b_ref[0]


def _modulation(cond, w_mod, b_mod, tn):
    depth, d, nm = w_mod.shape
    return pl.pallas_call(
        _mod_kernel,
        grid=(depth, nm // tn),
        in_specs=[
            pl.BlockSpec((MOD_ROWS, d), lambda l, j: (0, 0)),
            pl.BlockSpec((1, d, tn), lambda l, j: (l, 0, j)),
            pl.BlockSpec((1, 1, tn), lambda l, j: (l, 0, j)),
        ],
        out_specs=pl.BlockSpec((1, MOD_ROWS, tn), lambda l, j: (l, 0, j)),
        out_shape=jax.ShapeDtypeStruct((depth, MOD_ROWS, nm), F32),
        compiler_params=_params("parallel", "parallel"),
        name="modulation",
    )(cond, w_mod, b_mod.reshape(depth, 1, nm))


def _ffn_kernel(x_ref, mod_ref, gpre_ref, gpost_ref, wg_ref, wu_ref, wd_ref, o_ref, *, sub, fc):
    x = x_ref[...]
    h = _modulated(x, gpre_ref[...], mod_ref, sub).astype(BF16)
    d_ff = wg_ref.shape[1]
    y = jnp.zeros(x.shape, F32)
    for c0 in range(0, d_ff, fc):
        g = _dot(h, wg_ref[:, c0:c0 + fc])
        u = _dot(h, wu_ref[:, c0:c0 + fc])
        a = (_silu(g) * u).astype(BF16)
        y = y + _dot(a, wd_ref[c0:c0 + fc, :])
    gate = mod_ref[3 * sub + 2:3 * sub + 3, :]
    o_ref[...] = x + 0.5 * gate * _rms(y, gpost_ref[...])


def _half_ffn(x_all, mods, layer, sub, g_pre, g_post, wg, wu, wd, *, n_rows, tm, cond_of_tile, fc):
    d = x_all.shape[1]
    d_ff = wg.shape[1]
    return pl.pallas_call(
        functools.partial(_ffn_kernel, sub=sub, fc=fc),
        grid=(n_rows // tm,),
        in_specs=[
            pl.BlockSpec((tm, d), lambda i: (i, 0)),
            pl.BlockSpec((None, None, N_MOD, d), lambda i: (layer, cond_of_tile(i), 0, 0)),
            _const_spec((1, d)),
            _const_spec((1, d)),
            _const_spec((d, d_ff)),
            _const_spec((d, d_ff)),
            _const_spec((d_ff, d)),
        ],
        out_specs=pl.BlockSpec((tm, d), lambda i: (i, 0)),
        out_shape=jax.ShapeDtypeStruct((n_rows, d), F32),
        compiler_params=_params("parallel"),
        name=f"half_ffn_{sub}",
    )(x_all, mods, g_pre.reshape(1, d), g_post.reshape(1, d), wg, wu, wd)


_Z_GLA = 0
_Z_DEC = 1536
_Z_FOUR = 1664
_Z_CQ = 2176
_Z_CKV = 2560
_Z_KR = 2816
_Z_END = 2944


def _rope_slot(x, tab):
    lane = lax.broadcasted_iota(jnp.int32, x.shape, 1)
    in_rope = (lane >= MLA_NOPE) & (lane < MLA_NOPE + MLA_ROPE)
    c = jnp.where(in_rope, pltpu.roll(tab, 64, 1), 1.0)
    s1 = jnp.where(in_rope, pltpu.roll(tab, 32, 1), 0.0)
    s2 = jnp.where(in_rope, tab, 0.0)
    h = MLA_ROPE // 4
    return x * c + pltpu.roll(x, LANES - h, 1) * s1 + pltpu.roll(x, h, 1) * s2


def _inproj_kernel(x_ref, mod_ref, gpre_ref, win_ref, wdec_ref, bdec_ref, qn_ref, wuq_ref,
                   kvn_ref, wukv_ref, tab_ref, gla_ref, fz_ref, q_ref, k_ref, v_ref):
    h = _modulated(x_ref[...], gpre_ref[...], mod_ref, 1).astype(BF16)
    z = _dot(h, win_ref[...])
    nqk = GLA_HEADS * GLA_DK
    gla_ref[:, 0:nqk] = z[:, 0:nqk] * (GLA_DK ** -0.5)
    gla_ref[:, nqk:_Z_DEC] = z[:, nqk:_Z_DEC]
    xd = _dot(z[:, _Z_DEC:_Z_FOUR].astype(BF16), wdec_ref[...]) + bdec_ref[...]
    logsig = jnp.minimum(xd, 0.0) - jnp.log1p(jnp.exp(-jnp.abs(xd)))
    gla_ref[:, _Z_DEC:_Z_DEC + 2 * nqk] = logsig * (1.0 / GLA_TAU)
    fz_ref[...] = z[:, _Z_FOUR:_Z_CQ]
    tab = tab_ref[...]
    cq = _rms(z[:, _Z_CQ:_Z_CKV], qn_ref[...]).astype(BF16)
    q = _dot(cq, wuq_ref[...])
    ckv = _rms(z[:, _Z_CKV:_Z_KR], kvn_ref[...]).astype(BF16)
    kv = _dot(ckv, wukv_ref[...])
    nk = MLA_HEADS * MLA_SLOT
    kr = _rope_slot(pltpu.roll(z[:, _Z_KR:_Z_END], MLA_NOPE, 1), tab)
    scale = (MLA_NOPE + MLA_ROPE) ** -0.5
    for hd in range(MLA_HEADS):
        sl = slice(hd * MLA_SLOT, (hd + 1) * MLA_SLOT)
        q_ref[:, sl] = (_rope_slot(q[:, sl], tab) * scale).astype(BF16)
        k_ref[:, sl] = (kv[:, sl] + kr).astype(BF16)
    v_ref[...] = kv[:, nk:].astype(BF16)


def _in_projection(x_all, mods, layer, g_pre, w_in2, wdec, bdec, qn, wuq, kvn, wukv, tab,
                   *, tm, cond_of_tile, tab_of_tile):
    t, d = x_all.shape
    nq = MLA_HEADS * MLA_SLOT
    nv = MLA_HEADS * MLA_V
    row = lambda w: pl.BlockSpec((tm, w), lambda i: (i, 0))
    return pl.pallas_call(
        _inproj_kernel,
        grid=(t // tm,),
        in_specs=[
            row(d),
            pl.BlockSpec((None, None, N_MOD, d), lambda i: (layer, cond_of_tile(i), 0, 0)),
            _const_spec((1, d)),
            _const_spec(w_in2.shape),
            _const_spec(wdec.shape),
            _const_spec(bdec.shape),
            _const_spec(qn.shape),
            _const_spec(wuq.shape),
            _const_spec(kvn.shape),
            _const_spec(wukv.shape),
            pl.BlockSpec((tm, LANES), lambda i: (tab_of_tile(i), 0)),
        ],
        out_specs=[row(2048), row(512), row(nq), row(nq), row(nv)],
        out_shape=[
            jax.ShapeDtypeStruct((t, 2048), F32),
            jax.ShapeDtypeStruct((t, 512), F32),
            jax.ShapeDtypeStruct((t, nq), BF16),
            jax.ShapeDtypeStruct((t, nq), BF16),
            jax.ShapeDtypeStruct((t, nv), BF16),
        ],
        compiler_params=_params("parallel"),
        name="in_projection",
    )(x_all, mods, g_pre.reshape(1, d), w_in2, wdec, bdec, qn, wuq, kvn, wukv, tab)


def _gla_kernel(*refs, reverse, final):
    if final:
        g_ref, of_ref, gn_ref, o_ref, st_ref = refs
    else:
        g_ref, o_ref, st_ref = refs
    c = GLA_CHUNK
    tb = g_ref.shape[0]
    nqk = GLA_HEADS * GLA_DK

    @pl.when(pl.program_id(1) == 0)
    def _():
        st_ref[...] = jnp.zeros(st_ref.shape, F32)

    row = lax.broadcasted_iota(jnp.int32, (c, c), 0)
    col = lax.broadcasted_iota(jnp.int32, (c, c), 1)
    keep = (col >= row) if reverse else (col <= row)
    lmat = jnp.where(keep, 1.0, 0.0).astype(BF16)
    la_off = _Z_DEC + (nqk if reverse else 0)
    mid = c // 2 if reverse else c // 2 - 1
    end = 0 if reverse else c - 1
    states = [st_ref[hd] for hd in range(GLA_HEADS)]
    chunks = range(tb // c)
    for ci in (reversed(chunks) if reverse else chunks):
        rows = slice(ci * c, (ci + 1) * c)
        la = g_ref[rows, la_off:la_off + nqk]
        h1 = la.astype(BF16)
        r1 = la - h1.astype(F32)
        h2 = r1.astype(BF16)
        h3 = (r1 - h2.astype(F32)).astype(BF16)
        b = _dot(lmat, h1) + _dot(lmat, h2) + _dot(lmat, h3)
        ref = b[mid:mid + 1, :]
        tot = b[end:end + 1, :]
        qt = g_ref[rows, 0:nqk] * jnp.exp(b - ref)
        kt = g_ref[rows, nqk:2 * nqk] * jnp.exp(jnp.minimum(ref - b, GLA_EXP_CLAMP))
        q_in = (qt * jnp.exp(ref)).astype(BF16)
        k_st = (kt * jnp.exp(tot - ref)).astype(BF16)
        qt = qt.astype(BF16)
        kt = kt.astype(BF16)
        dec = jnp.exp(tot)
        for hd in range(GLA_HEADS):
            ks = slice(hd * GLA_DK, (hd + 1) * GLA_DK)
            vs = slice(hd * GLA_DV, (hd + 1) * GLA_DV)
            v = g_ref[rows, 2 * nqk + hd * GLA_DV:2 * nqk + (hd + 1) * GLA_DV].astype(BF16)
            a = lax.dot_general(qt[:, ks], kt[:, ks], NT, preferred_element_type=F32)
            a = jnp.where(keep, a, 0.0).astype(BF16)
            st = states[hd]
            o = _dot(a, v) + lax.dot_general(q_in[:, ks], st.astype(BF16), NT, preferred_element_type=F32)
            states[hd] = st * dec[:, ks] + lax.dot_general(v, k_st[:, ks], TN, preferred_element_type=F32)
            if final:
                o = o + of_ref[rows, vs]
                gate = g_ref[rows, 2 * nqk + GLA_HEADS * GLA_DV + hd * GLA_DV:
                             2 * nqk + GLA_HEADS * GLA_DV + (hd + 1) * GLA_DV]
                o_ref[rows, vs] = (_rms(o, gn_ref[...]) * _silu(gate)).astype(o_ref.dtype)
            else:
                o_ref[rows, vs] = o
    for hd in range(GLA_HEADS):
        st_ref[hd] = states[hd]


def _gla_scan(gla_in, o_fwd, g_norm, *, reverse, tb, batch, seq, ctx):
    t = gla_in.shape[0]
    n_lat, n_ctx = seq // tb, ctx // tb
    ctx0 = batch * seq // tb

    def blk(b, j):
        if reverse:
            return jnp.where(j < n_ctx, ctx0 + b * n_ctx + (n_ctx - 1 - j), b * n_lat + (n_lat + n_ctx - 1 - j))
        return jnp.where(j < n_ctx, ctx0 + b * n_ctx + j, b * n_lat + (j - n_ctx))

    final = o_fwd is not None
    dv = GLA_HEADS * GLA_DV
    in_specs = [pl.BlockSpec((tb, gla_in.shape[1]), lambda b, j: (blk(b, j), 0))]
    args = [gla_in]
    if final:
        in_specs += [pl.BlockSpec((tb, dv), lambda b, j: (blk(b, j), 0)), _const_spec((1, GLA_DV))]
        args += [o_fwd, g_norm.reshape(1, GLA_DV)]
    return pl.pallas_call(
        functools.partial(_gla_kernel, reverse=reverse, final=final),
        grid=(batch, n_lat + n_ctx),
        in_specs=in_specs,
        out_specs=pl.BlockSpec((tb, dv), lambda b, j: (blk(b, j), 0)),
        out_shape=jax.ShapeDtypeStruct((t, dv), BF16 if final else F32),
        scratch_shapes=[pltpu.VMEM((GLA_HEADS, GLA_DV, GLA_DK), F32)],
        compiler_params=_params("parallel", "arbitrary"),
        name="gla_bwd_scan" if reverse else "gla_fwd_scan",
    )(*args)


def _dft_mats(n):
    k = jnp.arange(n, dtype=jnp.int32)
    ang = (2.0 * jnp.pi / n) * ((k[:, None] * k[None, :]) % n).astype(F32)
    return jnp.cos(ang), jnp.sin(ang)


def _fft_a_kernel(x_ref, f_ref, z_ref):
    na = x_ref.shape[0]
    z = _dot(f_ref[...], x_ref[...].astype(BF16))
    z_ref[0] = z[:na]
    z_ref[1] = z[na:]


def _fft_b_kernel(z_ref, twc_ref, tws_ref, m_ref, cs_ref, o_ref):
    nb = z_ref.shape[2]
    for j in range(z_ref.shape[1]):
        zr, zi = z_ref[0, j], z_ref[1, j]
        c, s = twc_ref[j], tws_ref[j]
        tw = jnp.concatenate([zr * c + zi * s, zi * c - zr * s], axis=0).astype(BF16)
        v = _dot(m_ref[...], tw)
        vv = jnp.concatenate([v[:nb], v[nb:]], axis=1).astype(BF16)
        o_ref[:, j, :] = _dot(vv, cs_ref[...])


def _fourier_latent(fz, out_rows, *, batch, seq, kab, lb):
    t, cw = fz.shape
    na, nb = FFT_NA, seq // FFT_NA
    ch = FOURIER_CH
    ca, sa = _dft_mats(na)
    f_a = jnp.concatenate([ca, -sa], axis=0).astype(BF16)
    z = pl.pallas_call(
        _fft_a_kernel,
        grid=(batch, nb * cw // lb),
        in_specs=[pl.BlockSpec((na, lb), lambda b, j: (b, j)), _const_spec((2 * na, na))],
        out_specs=pl.BlockSpec((2, na, lb), lambda b, j: (0, b, j)),
        out_shape=jax.ShapeDtypeStruct((2, batch * na, nb * cw), F32),
        compiler_params=_params("parallel", "parallel"),
        name="fnet_dft_rows",
    )(fz.reshape(t // nb, nb * cw), f_a)

    cb, sb = _dft_mats(nb)
    m_b = jnp.concatenate([jnp.concatenate([cb, sb], axis=1),
                           jnp.concatenate([-sb, cb], axis=1)], axis=0).astype(BF16)
    cc, sc = _dft_mats(ch)
    cs = (jnp.concatenate([cc, sc], axis=0) * ((seq * ch) ** -0.5)).astype(BF16)
    ka = jnp.arange(na, dtype=jnp.int32)[:, None]
    bb = jnp.arange(nb, dtype=jnp.int32)[None, :]
    ang = (2.0 * jnp.pi / seq) * ((ka * bb) % seq).astype(F32)
    twc = jnp.broadcast_to(jnp.cos(ang)[:, :, None], (na, nb, ch))
    tws = jnp.broadcast_to(jnp.sin(ang)[:, :, None], (na, nb, ch))
    nab = na // kab
    out = pl.pallas_call(
        _fft_b_kernel,
        grid=(batch, cw // ch, nab),
        in_specs=[
            pl.BlockSpec((2, kab, nb, ch), lambda b, g, i: (0, b * nab + i, 0, g)),
            pl.BlockSpec((kab, nb, ch), lambda b, g, i: (i, 0, 0)),
            pl.BlockSpec((kab, nb, ch), lambda b, g, i: (i, 0, 0)),
            _const_spec((2 * nb, 2 * nb)),
            _const_spec((2 * ch, ch)),
        ],
        out_specs=pl.BlockSpec((nb, kab, ch), lambda b, g, i: (b, i, g)),
        out_shape=jax.ShapeDtypeStruct((out_rows // na, na, cw), F32),
        compiler_params=_params("parallel", "parallel", "parallel"),
        name="fnet_dft_cols",
    )(z.reshape(2, batch * na, nb, cw), twc, tws, m_b, cs)
    return out.reshape(out_rows, cw)


def _fft_ctx_kernel(x_ref, yb_hbm_ref, f_ref, cs_ref, o_ref):
    del yb_hbm_ref
    ch = FOURIER_CH
    f = f_ref[...]
    for g in range(x_ref.shape[1] // ch):
        ab = _dot(x_ref[:, g * ch:(g + 1) * ch].astype(BF16), cs_ref[...])
        st = jnp.concatenate([ab[:, :ch], ab[:, ch:]], axis=0).astype(BF16)
        o_ref[:, g * ch:(g + 1) * ch] = _dot(f, st)


def _fourier_context(fz, yb, *, batch, seq, ctx):
    cw = fz.shape[1]
    ch = FOURIER_CH
    cn, sn = _dft_mats(ctx)
    f = (jnp.concatenate([cn, -sn], axis=1) * ((ctx * ch) ** -0.5)).astype(BF16)
    cc, sc = _dft_mats(ch)
    cs = jnp.concatenate([cc, sc], axis=1).astype(BF16)
    blk0 = batch * seq // ctx
    return pl.pallas_call(
        _fft_ctx_kernel,
        grid=(batch,),
        in_specs=[
            pl.BlockSpec((ctx, cw), lambda b: (blk0 + b, 0)),
            pl.BlockSpec(memory_space=pl.ANY),
            _const_spec(f.shape),
            _const_spec(cs.shape),
        ],
        out_specs=pl.BlockSpec((ctx, cw), lambda b: (blk0 + b, 0)),
        out_shape=jax.ShapeDtypeStruct(yb.shape, yb.dtype),
        input_output_aliases={1: 0},
        compiler_params=_params("parallel"),
        name="fnet_context",
    )(fz, yb, f, cs)


def _attn_step(q, k, v, carry):
    m, l, acc = carry
    s = lax.dot_general(q, k, NT, preferred_element_type=F32)
    m_new = jnp.maximum(m, jnp.max(s, axis=-1, keepdims=True))
    alpha = jnp.exp(m - m_new)
    p = jnp.exp(s - m_new)
    l = alpha * l + jnp.sum(p, axis=-1, keepdims=True)
    acc = alpha * acc + _dot(p.astype(BF16), v)
    return m_new, l, acc


def _attn_kernel(*refs, kc, latent):
    if latent:
        q_ref, kl_ref, vl_ref, kc_ref, vc_ref, yc_hbm_ref, o_ref = refs
    else:
        q_ref, kc_ref, vc_ref, yc_hbm_ref, o_ref = refs
    del yc_hbm_ref
    tq = q_ref.shape[0]
    sl = [slice(0, MLA_SLOT), slice(MLA_SLOT, 2 * MLA_SLOT)]
    qs = [q_ref[:, s] for s in sl]
    init = (jnp.full((tq, 1), -jnp.inf, F32), jnp.zeros((tq, 1), F32), jnp.zeros((tq, 2 * MLA_V), F32))
    carry = (init, init)
    if latent:
        def body(c, carry):
            rows = pl.ds(pl.multiple_of(c * kc, kc), kc)
            v = vl_ref[rows, :]
            return tuple(_attn_step(qs[h], kl_ref[rows, sl[h]], v, carry[h]) for h in range(2))
        carry = lax.fori_loop(0, kl_ref.shape[0] // kc, body, carry)
    v = vc_ref[...]
    carry = tuple(_attn_step(qs[h], kc_ref[:, sl[h]], v, carry[h]) for h in range(2))
    outs = [acc / l for (_, l, acc) in carry]
    lane = lax.broadcasted_iota(jnp.int32, outs[0].shape, 1)
    o_ref[...] = jnp.where(lane < MLA_V, outs[0], outs[1]).astype(o_ref.dtype)


def _attention(q, k, v, yc, *, latent, batch, seq, ctx, tq, kc):
    hp = MLA_HEADS // 2
    w = 2 * MLA_SLOT
    wv = 2 * MLA_V
    ctx0 = batch * seq // ctx
    kc_spec = pl.BlockSpec((ctx, w), lambda b, h, i: (ctx0 + b, h))
    vc_spec = pl.BlockSpec((ctx, wv), lambda b, h, i: (ctx0 + b, h))
    if latent:
        nq = seq // tq
        qrow = lambda b, i: b * nq + i
        in_specs = [
            pl.BlockSpec((tq, w), lambda b, h, i: (qrow(b, i), h)),
            pl.BlockSpec((seq, w), lambda b, h, i: (b, h)),
            pl.BlockSpec((seq, wv), lambda b, h, i: (b, h)),
            kc_spec, vc_spec,
        ]
        args = [q, k, v, k, v]
    else:
        nq = ctx // tq
        q0 = batch * seq // tq
        qrow = lambda b, i: q0 + b * nq + i
        in_specs = [pl.BlockSpec((tq, w), lambda b, h, i: (qrow(b, i), h)), kc_spec, vc_spec]
        args = [q, k, v]
    n_in = len(args)
    return pl.pallas_call(
        functools.partial(_attn_kernel, kc=kc, latent=latent),
        grid=(batch, hp, nq),
        in_specs=in_specs + [pl.BlockSpec(memory_space=pl.ANY)],
        out_specs=pl.BlockSpec((tq, wv), lambda b, h, i: (qrow(b, i), h)),
        out_shape=jax.ShapeDtypeStruct(yc.shape, yc.dtype),
        input_output_aliases={n_in: 0},
        compiler_params=_params("parallel", "parallel", "arbitrary"),
        name="mla_latent" if latent else "mla_context",
    )(*args, yc)


def _merge_kernel(x_ref, mod_ref, gpre_ref, gpost_ref, wgt_ref, ya_ref, yb_ref, yc_ref, wb_ref, wo_ref, o_ref):
    x = x_ref[...]
    d = x.shape[1]
    h = _modulated(x, gpre_ref[...], mod_ref, 1).astype(BF16)
    ys = (ya_ref[...], yb_ref[...].astype(BF16), yc_ref[...])
    m = jnp.zeros(x.shape, F32)
    for br in range(3):
        gate = jax.nn.sigmoid(_dot(h, wgt_ref[:, br * d:(br + 1) * d]))
        m = m + gate * _dot(ys[br], wb_ref[br])
    out = _dot(m.astype(BF16), wo_ref[...])
    o_ref[...] = x + mod_ref[5:6, :] * _rms(out, gpost_ref[...])


def _merge(x_all, mods, layer, g_pre, g_post, w_gates, ya, yb, yc, wb, wo, *, n_rows, tm, cond_of_tile):
    d = x_all.shape[1]
    bw = ya.shape[1]
    row = lambda w: pl.BlockSpec((tm, w), lambda i: (i, 0))
    return pl.pallas_call(
        _merge_kernel,
        grid=(n_rows // tm,),
        in_specs=[
            row(d),
            pl.BlockSpec((None, None, N_MOD, d), lambda i: (layer, cond_of_tile(i), 0, 0)),
            _const_spec((1, d)),
            _const_spec((1, d)),
            _const_spec(w_gates.shape),
            row(bw), row(bw), row(bw),
            _const_spec(wb.shape),
            _const_spec(wo.shape),
        ],
        out_specs=row(d),
        out_shape=jax.ShapeDtypeStruct((n_rows, d), F32),
        compiler_params=_params("parallel"),
        name="merge",
    )(x_all, mods, g_pre.reshape(1, d), g_post.reshape(1, d), w_gates, ya, yb, yc, wb, wo)


def _rope_table(seq, tm):
    pos = jnp.arange(seq, dtype=jnp.int32)
    rows = (pos // GRID_W).astype(F32)
    cols = (pos % GRID_W).astype(F32)
    half = MLA_ROPE // 2
    inv_freq = ROPE_BASE ** (-jnp.arange(0, half, 2, dtype=F32) / half)
    ar, ac = rows[:, None] * inv_freq, cols[:, None] * inv_freq
    z = jnp.zeros_like(ar)
    cos = jnp.concatenate([jnp.cos(ar), jnp.cos(ar), jnp.cos(ac), jnp.cos(ac)], axis=1)
    s1 = jnp.concatenate([-jnp.sin(ar), z, -jnp.sin(ac), z], axis=1)
    s2 = jnp.concatenate([z, jnp.sin(ar), z, jnp.sin(ac)], axis=1)
    tab = jnp.concatenate([cos, s1, s2, jnp.zeros((seq, LANES - 3 * MLA_ROPE), F32)], axis=1)
    ident = jnp.concatenate([jnp.ones((tm, MLA_ROPE), F32), jnp.zeros((tm, LANES - MLA_ROPE), F32)], axis=1)
    return jnp.concatenate([tab, ident], axis=0)


def _pad_cols(w, n):
    return jnp.pad(w, ((0, 0), (0, n - w.shape[1])))


def _layer_weights(layer, w_in, gla_w_decay, gla_b_decay, mla_w_uq, mla_w_ukv):
    nqk = GLA_HEADS * GLA_DK
    r = GLA_GATE_RANK
    o_dec = 2 * nqk + 2 * GLA_HEADS * GLA_DV
    o_four = o_dec + 2 * r
    o_cq = o_four + FOURIER_GROUPS * FOURIER_CH
    q_rank = mla_w_uq.shape[1]
    kv_rank = mla_w_ukv.shape[1]
    o_ckv = o_cq + q_rank
    o_kr = o_ckv + kv_rank
    o_gates = o_kr + MLA_ROPE
    w = w_in[layer]
    w_in2 = jnp.concatenate([
        w[:, :o_dec], _pad_cols(w[:, o_dec:o_four], LANES), w[:, o_four:o_cq], w[:, o_cq:o_ckv],
        w[:, o_ckv:o_kr], _pad_cols(w[:, o_kr:o_gates], LANES)], axis=1).astype(BF16)
    w_gates = w[:, o_gates:].astype(BF16)
    wd = gla_w_decay[layer]
    wdec = jnp.zeros((LANES, 2 * nqk), F32)
    wdec = wdec.at[0:r, 0:nqk].set(wd[0]).at[r:2 * r, nqk:].set(wd[1]).astype(BF16)
    bdec = gla_b_decay[layer].reshape(1, 2 * nqk)
    dq = MLA_NOPE + MLA_ROPE
    wuq = mla_w_uq[layer].reshape(q_rank, MLA_HEADS, dq)
    wuq = jnp.pad(wuq, ((0, 0), (0, 0), (0, MLA_SLOT - dq))).reshape(q_rank, MLA_HEADS * MLA_SLOT).astype(BF16)
    wkv = mla_w_ukv[layer].reshape(kv_rank, MLA_HEADS, MLA_NOPE + MLA_V)
    wk = jnp.pad(wkv[:, :, :MLA_NOPE], ((0, 0), (0, 0), (0, MLA_SLOT - MLA_NOPE)))
    wukv = jnp.concatenate([wk.reshape(kv_rank, MLA_HEADS * MLA_SLOT),
                            wkv[:, :, MLA_NOPE:].reshape(kv_rank, MLA_HEADS * MLA_V)], axis=1).astype(BF16)
    return w_in2, w_gates, wdec, bdec, wuq, wukv


def kernel(x, c, ctx, c_ctx, w_mod, b_mod, norm_pre, norm_post, ffn_w_gate, ffn_w_up, ffn_w_down, w_in,
           gla_w_decay, gla_b_decay, gla_norm, mla_q_norm, mla_w_uq, mla_kv_norm, mla_w_ukv, w_branch, w_out):
    batch, seq, d = x.shape
    n_ctx = ctx.shape[1]
    depth = w_mod.shape[0]
    n_lat = batch * seq
    t = n_lat + batch * n_ctx
    tm = min(256, n_ctx)
    d_ff = ffn_w_gate.shape[-1]
    fc = d_ff // 2 if (d_ff // 2) % LANES == 0 else d_ff
    assert seq % tm == 0 and n_ctx % tm == 0 and seq % (FFT_NA * 8) == 0 and batch < MOD_ROWS

    cond_of_tile = lambda i: jnp.minimum((i * tm) // seq, batch)
    tab_of_tile = lambda i: jnp.where(i * tm < n_lat, ((i * tm) % seq) // tm, seq // tm)

    cond = jnp.concatenate([c, c_ctx[None, :], jnp.zeros((MOD_ROWS - batch - 1, d), F32)], axis=0)
    mods = _modulation(cond, w_mod, b_mod, tn=(N_MOD * d) // 8).reshape(depth, MOD_ROWS, N_MOD, d)
    tab = _rope_table(seq, tm)
    x_all = jnp.concatenate([x.reshape(n_lat, d), ctx.reshape(batch * n_ctx, d)], axis=0)

    for layer in range(depth):
        last = layer == depth - 1
        ffn = lambda xa, sub, s, rows: _half_ffn(
            xa, mods, layer, sub, norm_pre[layer, sub], norm_post[layer, sub],
            ffn_w_gate[layer, s].astype(BF16), ffn_w_up[layer, s].astype(BF16), ffn_w_down[layer, s].astype(BF16),
            n_rows=rows, tm=tm, cond_of_tile=cond_of_tile, fc=fc)
        x_all = ffn(x_all, 0, 0, t)

        w_in2, w_gates, wdec, bdec, wuq, wukv = _layer_weights(
            layer, w_in, gla_w_decay, gla_b_decay, mla_w_uq, mla_w_ukv)
        gla_in, fz, q, k, v = _in_projection(
            x_all, mods, layer, norm_pre[layer, 1], w_in2, wdec, bdec,
            mla_q_norm[layer].reshape(1, -1), wuq, mla_kv_norm[layer].reshape(1, -1), wukv, tab,
            tm=tm, cond_of_tile=cond_of_tile, tab_of_tile=tab_of_tile)

        scan = functools.partial(_gla_scan, tb=tm, batch=batch, seq=seq, ctx=n_ctx)
        o_fwd = scan(gla_in, None, None, reverse=False)
        ya = scan(gla_in, o_fwd, gla_norm[layer], reverse=True)

        rows_out = n_lat if last else t
        yb = _fourier_latent(fz, rows_out, batch=batch, seq=seq, kab=16, lb=4096)
        yc = jnp.zeros((rows_out, MLA_HEADS * MLA_V), BF16)
        yc = _attention(q, k, v, yc, latent=True, batch=batch, seq=seq, ctx=n_ctx, tq=tm, kc=256)
        if not last:
            yb = _fourier_context(fz, yb, batch=batch, seq=seq, ctx=n_ctx)
            yc = _attention(q, k, v, yc, latent=False, batch=batch, seq=seq, ctx=n_ctx, tq=tm, kc=256)

        x_all = _merge(x_all, mods, layer, norm_pre[layer, 1], norm_post[layer, 1], w_gates, ya, yb, yc,
                       w_branch[layer].astype(BF16), w_out[layer].astype(BF16),
                       n_rows=rows_out, tm=tm, cond_of_tile=cond_of_tile)
        x_all = ffn(x_all, 2, 1, rows_out)
    return x_all.reshape(batch, seq, d)
```

```python
import functools

import jax
import jax.numpy as jnp
from jax import lax
from jax.experimental import pallas as pl
from jax.experimental.pallas import tpu as pltpu

F32 = jnp.float32
BF16 = jnp.bfloat16

GRID_W = 64
N_MOD = 9
GLA_HEADS = 4
GLA_DK = 64
GLA_DV = 128
GLA_GATE_RANK = 16
GLA_TAU = 16.0
GLA_CHUNK = 64
GLA_EXP_CLAMP = 80.0
FOURIER_GROUPS = 4
FOURIER_CH = 128
FFT_NA = 64
MLA_HEADS = 8
MLA_NOPE = 64
MLA_ROPE = 32
MLA_V = 64
MLA_SLOT = 128
ROPE_BASE = 10000.0
EPS = 1e-6
LOG2_E = 1.4426950408889634
ATTN_TQ = 512
ATTN_KC = 1024
LANES = 128
MOD_ROWS = 8
VMEM_LIMIT = 56 * 1024 * 1024

NT = (((1,), (1,)), ((), ()))
TN = (((0,), (0,)), ((), ()))


def _params(*sem):
    return pltpu.CompilerParams(dimension_semantics=sem, vmem_limit_bytes=VMEM_LIMIT)


def _const_spec(shape):
    nd = len(shape)
    return pl.BlockSpec(shape, lambda *_: (0,) * nd, pipeline_mode=pl.Buffered(1))


def _dot(a, b):
    return jnp.dot(a, b, preferred_element_type=F32)


def _rms(x, g):
    return x * lax.rsqrt(jnp.mean(x * x, axis=-1, keepdims=True) + EPS) * g


def _silu(x):
    return x * jax.nn.sigmoid(x)


def _modulated(x, g, mod_ref, sub):
    shift = mod_ref[3 * sub + 0:3 * sub + 1, :]
    scale = mod_ref[3 * sub + 1:3 * sub + 2, :]
    return _rms(x, g) * (1.0 + scale) + shift


def _mod_kernel(c_ref, w_ref, b_ref, o_ref):
    a = _silu(c_ref[...]).astype(BF16)
    o_ref[0] = _dot(a, w_ref[0].astype(BF16)) + b_ref[0]


def _modulation(cond, w_mod, b_mod, tn):
    depth, d, nm = w_mod.shape
    return pl.pallas_call(
        _mod_kernel,
        grid=(depth, nm // tn),
        in_specs=[
            pl.BlockSpec((MOD_ROWS, d), lambda l, j: (0, 0)),
            pl.BlockSpec((1, d, tn), lambda l, j: (l, 0, j)),
            pl.BlockSpec((1, 1, tn), lambda l, j: (l, 0, j)),
        ],
        out_specs=pl.BlockSpec((1, MOD_ROWS, tn), lambda l, j: (l, 0, j)),
        out_shape=jax.ShapeDtypeStruct((depth, MOD_ROWS, nm), F32),
        compiler_params=_params("parallel", "parallel"),
        name="modulation",
    )(cond, w_mod, b_mod.reshape(depth, 1, nm))


def _ffn_kernel(x_ref, mod_ref, gpre_ref, gpost_ref, wg_ref, wu_ref, wd_ref, o_ref, *, sub, fc):
    x = x_ref[...]
    h = _modulated(x, gpre_ref[...], mod_ref, sub).astype(BF16)
    d_ff = wg_ref.shape[1]
    y = jnp.zeros(x.shape, F32)
    for c0 in range(0, d_ff, fc):
        g = _dot(h, wg_ref[:, c0:c0 + fc])
        u = _dot(h, wu_ref[:, c0:c0 + fc])
        a = (_silu(g) * u).astype(BF16)
        y = y + _dot(a, wd_ref[c0:c0 + fc, :])
    gate = mod_ref[3 * sub + 2:3 * sub + 3, :]
    o_ref[...] = x + 0.5 * gate * _rms(y, gpost_ref[...])


def _half_ffn(x_all, mods, layer, sub, g_pre, g_post, wg, wu, wd, *, n_rows, tm, cond_of_tile, fc):
    d = x_all.shape[1]
    d_ff = wg.shape[1]
    return pl.pallas_call(
        functools.partial(_ffn_kernel, sub=sub, fc=fc),
        grid=(n_rows // tm,),
        in_specs=[
            pl.BlockSpec((tm, d), lambda i: (i, 0)),
            pl.BlockSpec((None, None, N_MOD, d), lambda i: (layer, cond_of_tile(i), 0, 0)),
            _const_spec((1, d)),
            _const_spec((1, d)),
            _const_spec((d, d_ff)),
            _const_spec((d, d_ff)),
            _const_spec((d_ff, d)),
        ],
        out_specs=pl.BlockSpec((tm, d), lambda i: (i, 0)),
        out_shape=jax.ShapeDtypeStruct((n_rows, d), F32),
        compiler_params=_params("parallel"),
        name=f"half_ffn_{sub}",
    )(x_all, mods, g_pre.reshape(1, d), g_post.reshape(1, d), wg, wu, wd)


_Z_GLA = 0
_Z_DEC = 1536
_Z_FOUR = 1664
_Z_CQ = 2176
_Z_CKV = 2560
_Z_KR = 2816
_Z_END = 2944


def _rope_slot(x, tab):
    lane = lax.broadcasted_iota(jnp.int32, x.shape, 1)
    in_rope = (lane >= MLA_NOPE) & (lane < MLA_NOPE + MLA_ROPE)
    c = jnp.where(in_rope, pltpu.roll(tab, 64, 1), 1.0)
    s1 = jnp.where(in_rope, pltpu.roll(tab, 32, 1), 0.0)
    s2 = jnp.where(in_rope, tab, 0.0)
    h = MLA_ROPE // 4
    return x * c + pltpu.roll(x, LANES - h, 1) * s1 + pltpu.roll(x, h, 1) * s2


def _inproj_kernel(x_ref, mod_ref, gpre_ref, win_ref, wdec_ref, bdec_ref, qn_ref, wuq_ref,
                   kvn_ref, wukv_ref, tab_ref, gla_ref, fz_ref, q_ref, k_ref, v_ref):
    h = _modulated(x_ref[...], gpre_ref[...], mod_ref, 1).astype(BF16)
    z = _dot(h, win_ref[...])
    nqk = GLA_HEADS * GLA_DK
    gla_ref[:, 0:nqk] = z[:, 0:nqk] * (GLA_DK ** -0.5)
    gla_ref[:, nqk:_Z_DEC] = z[:, nqk:_Z_DEC]
    xd = _dot(z[:, _Z_DEC:_Z_FOUR].astype(BF16), wdec_ref[...]) + bdec_ref[...]
    logsig = jnp.minimum(xd, 0.0) - jnp.log1p(jnp.exp(-jnp.abs(xd)))
    gla_ref[:, _Z_DEC:_Z_DEC + 2 * nqk] = logsig * (1.0 / GLA_TAU)
    fz_ref[...] = z[:, _Z_FOUR:_Z_CQ]
    tab = tab_ref[...]
    cq = _rms(z[:, _Z_CQ:_Z_CKV], qn_ref[...]).astype(BF16)
    q = _dot(cq, wuq_ref[...])
    ckv = _rms(z[:, _Z_CKV:_Z_KR], kvn_ref[...]).astype(BF16)
    kv = _dot(ckv, wukv_ref[...])
    nk = MLA_HEADS * MLA_SLOT
    kr = _rope_slot(pltpu.roll(z[:, _Z_KR:_Z_END], MLA_NOPE, 1), tab)
    scale = (MLA_NOPE + MLA_ROPE) ** -0.5 * LOG2_E
    for hd in range(MLA_HEADS):
        sl = slice(hd * MLA_SLOT, (hd + 1) * MLA_SLOT)
        q_ref[:, sl] = (_rope_slot(q[:, sl], tab) * scale).astype(BF16)
        k_ref[:, sl] = (kv[:, sl] + kr).astype(BF16)
    v_ref[...] = kv[:, nk:].astype(BF16)


def _in_projection(x_all, mods, layer, g_pre, w_in2, wdec, bdec, qn, wuq, kvn, wukv, tab,
                   *, tm, cond_of_tile, tab_of_tile):
    t, d = x_all.shape
    nq = MLA_HEADS * MLA_SLOT
    nv = MLA_HEADS * MLA_V
    row = lambda w: pl.BlockSpec((tm, w), lambda i: (i, 0))
    return pl.pallas_call(
        _inproj_kernel,
        grid=(t // tm,),
        in_specs=[
            row(d),
            pl.BlockSpec((None, None, N_MOD, d), lambda i: (layer, cond_of_tile(i), 0, 0)),
            _const_spec((1, d)),
            _const_spec(w_in2.shape),
            _const_spec(wdec.shape),
            _const_spec(bdec.shape),
            _const_spec(qn.shape),
            _const_spec(wuq.shape),
            _const_spec(kvn.shape),
            _const_spec(wukv.shape),
            pl.BlockSpec((tm, LANES), lambda i: (tab_of_tile(i), 0)),
        ],
        out_specs=[row(2048), row(512), row(nq), row(nq), row(nv)],
        out_shape=[
            jax.ShapeDtypeStruct((t, 2048), F32),
            jax.ShapeDtypeStruct((t, 512), F32),
            jax.ShapeDtypeStruct((t, nq), BF16),
            jax.ShapeDtypeStruct((t, nq), BF16),
            jax.ShapeDtypeStruct((t, nv), BF16),
        ],
        compiler_params=_params("parallel"),
        name="in_projection",
    )(x_all, mods, g_pre.reshape(1, d), w_in2, wdec, bdec, qn, wuq, kvn, wukv, tab)


def _gla_kernel(*refs, reverse, final):
    if final:
        g_ref, of_ref, gn_ref, o_ref, st_ref = refs
    else:
        g_ref, o_ref, st_ref = refs
    c = GLA_CHUNK
    tb = g_ref.shape[0]
    nqk = GLA_HEADS * GLA_DK

    @pl.when(pl.program_id(1) == 0)
    def _():
        st_ref[...] = jnp.zeros(st_ref.shape, F32)

    row = lax.broadcasted_iota(jnp.int32, (c, c), 0)
    col = lax.broadcasted_iota(jnp.int32, (c, c), 1)
    keep = (col >= row) if reverse else (col <= row)
    lmat = jnp.where(keep, 1.0, 0.0).astype(BF16)
    la_off = _Z_DEC + (nqk if reverse else 0)
    mid = c // 2 if reverse else c // 2 - 1
    end = 0 if reverse else c - 1
    states = [st_ref[hd] for hd in range(GLA_HEADS)]
    chunks = range(tb // c)
    for ci in (reversed(chunks) if reverse else chunks):
        rows = slice(ci * c, (ci + 1) * c)
        la = g_ref[rows, la_off:la_off + nqk]
        h1 = la.astype(BF16)
        r1 = la - h1.astype(F32)
        h2 = r1.astype(BF16)
        h3 = (r1 - h2.astype(F32)).astype(BF16)
        b = _dot(lmat, h1) + _dot(lmat, h2) + _dot(lmat, h3)
        ref = b[mid:mid + 1, :]
        tot = b[end:end + 1, :]
        qt = g_ref[rows, 0:nqk] * jnp.exp(b - ref)
        kt = g_ref[rows, nqk:2 * nqk] * jnp.exp(jnp.minimum(ref - b, GLA_EXP_CLAMP))
        q_in = (qt * jnp.exp(ref)).astype(BF16)
        k_st = (kt * jnp.exp(tot - ref)).astype(BF16)
        qt = qt.astype(BF16)
        kt = kt.astype(BF16)
        dec = jnp.exp(tot)
        for hd in range(GLA_HEADS):
            ks = slice(hd * GLA_DK, (hd + 1) * GLA_DK)
            vs = slice(hd * GLA_DV, (hd + 1) * GLA_DV)
            v = g_ref[rows, 2 * nqk + hd * GLA_DV:2 * nqk + (hd + 1) * GLA_DV].astype(BF16)
            a = lax.dot_general(qt[:, ks], kt[:, ks], NT, preferred_element_type=F32)
            a = jnp.where(keep, a, 0.0).astype(BF16)
            st = states[hd]
            o = _dot(a, v) + lax.dot_general(q_in[:, ks], st.astype(BF16), NT, preferred_element_type=F32)
            states[hd] = st * dec[:, ks] + lax.dot_general(v, k_st[:, ks], TN, preferred_element_type=F32)
            if final:
                o = o + of_ref[rows, vs]
                gate = g_ref[rows, 2 * nqk + GLA_HEADS * GLA_DV + hd * GLA_DV:
                             2 * nqk + GLA_HEADS * GLA_DV + (hd + 1) * GLA_DV]
                o_ref[rows, vs] = (_rms(o, gn_ref[...]) * _silu(gate)).astype(o_ref.dtype)
            else:
                o_ref[rows, vs] = o
    for hd in range(GLA_HEADS):
        st_ref[hd] = states[hd]


def _gla_scan(gla_in, o_fwd, g_norm, *, reverse, tb, batch, seq, ctx):
    t = gla_in.shape[0]
    n_lat, n_ctx = seq // tb, ctx // tb
    ctx0 = batch * seq // tb

    def blk(b, j):
        if reverse:
            return jnp.where(j < n_ctx, ctx0 + b * n_ctx + (n_ctx - 1 - j), b * n_lat + (n_lat + n_ctx - 1 - j))
        return jnp.where(j < n_ctx, ctx0 + b * n_ctx + j, b * n_lat + (j - n_ctx))

    final = o_fwd is not None
    dv = GLA_HEADS * GLA_DV
    in_specs = [pl.BlockSpec((tb, gla_in.shape[1]), lambda b, j: (blk(b, j), 0))]
    args = [gla_in]
    if final:
        in_specs += [pl.BlockSpec((tb, dv), lambda b, j: (blk(b, j), 0)), _const_spec((1, GLA_DV))]
        args += [o_fwd, g_norm.reshape(1, GLA_DV)]
    return pl.pallas_call(
        functools.partial(_gla_kernel, reverse=reverse, final=final),
        grid=(batch, n_lat + n_ctx),
        in_specs=in_specs,
        out_specs=pl.BlockSpec((tb, dv), lambda b, j: (blk(b, j), 0)),
        out_shape=jax.ShapeDtypeStruct((t, dv), BF16 if final else F32),
        scratch_shapes=[pltpu.VMEM((GLA_HEADS, GLA_DV, GLA_DK), F32)],
        compiler_params=_params("parallel", "arbitrary"),
        name="gla_bwd_scan" if reverse else "gla_fwd_scan",
    )(*args)


def _dft_mats(n):
    k = jnp.arange(n, dtype=jnp.int32)
    ang = (2.0 * jnp.pi / n) * ((k[:, None] * k[None, :]) % n).astype(F32)
    return jnp.cos(ang), jnp.sin(ang)


def _fft_a_kernel(x_ref, f_ref, z_ref):
    na = x_ref.shape[0]
    z = _dot(f_ref[...], x_ref[...].astype(BF16))
    z_ref[0] = z[:na]
    z_ref[1] = z[na:]


def _fft_b_kernel(z_ref, twc_ref, tws_ref, m_ref, cs_ref, o_ref):
    nb = z_ref.shape[2]
    for j in range(z_ref.shape[1]):
        zr, zi = z_ref[0, j], z_ref[1, j]
        c, s = twc_ref[j], tws_ref[j]
        tw = jnp.concatenate([zr * c + zi * s, zi * c - zr * s], axis=0).astype(BF16)
        v = _dot(m_ref[...], tw)
        vv = jnp.concatenate([v[:nb], v[nb:]], axis=1).astype(BF16)
        o_ref[:, j, :] = _dot(vv, cs_ref[...])


def _fourier_latent(fz, out_rows, *, batch, seq, kab, lb):
    t, cw = fz.shape
    na, nb = FFT_NA, seq // FFT_NA
    ch = FOURIER_CH
    ca, sa = _dft_mats(na)
    f_a = jnp.concatenate([ca, -sa], axis=0).astype(BF16)
    z = pl.pallas_call(
        _fft_a_kernel,
        grid=(batch, nb * cw // lb),
        in_specs=[pl.BlockSpec((na, lb), lambda b, j: (b, j)), _const_spec((2 * na, na))],
        out_specs=pl.BlockSpec((2, na, lb), lambda b, j: (0, b, j)),
        out_shape=jax.ShapeDtypeStruct((2, batch * na, nb * cw), F32),
        compiler_params=_params("parallel", "parallel"),
        name="fnet_dft_rows",
    )(fz.reshape(t // nb, nb * cw), f_a)

    cb, sb = _dft_mats(nb)
    m_b = jnp.concatenate([jnp.concatenate([cb, sb], axis=1),
                           jnp.concatenate([-sb, cb], axis=1)], axis=0).astype(BF16)
    cc, sc = _dft_mats(ch)
    cs = (jnp.concatenate([cc, sc], axis=0) * ((seq * ch) ** -0.5)).astype(BF16)
    ka = jnp.arange(na, dtype=jnp.int32)[:, None]
    bb = jnp.arange(nb, dtype=jnp.int32)[None, :]
    ang = (2.0 * jnp.pi / seq) * ((ka * bb) % seq).astype(F32)
    twc = jnp.broadcast_to(jnp.cos(ang)[:, :, None], (na, nb, ch))
    tws = jnp.broadcast_to(jnp.sin(ang)[:, :, None], (na, nb, ch))
    nab = na // kab
    out = pl.pallas_call(
        _fft_b_kernel,
        grid=(batch, cw // ch, nab),
        in_specs=[
            pl.BlockSpec((2, kab, nb, ch), lambda b, g, i: (0, b * nab + i, 0, g)),
            pl.BlockSpec((kab, nb, ch), lambda b, g, i: (i, 0, 0)),
            pl.BlockSpec((kab, nb, ch), lambda b, g, i: (i, 0, 0)),
            _const_spec((2 * nb, 2 * nb)),
            _const_spec((2 * ch, ch)),
        ],
        out_specs=pl.BlockSpec((nb, kab, ch), lambda b, g, i: (b, i, g)),
        out_shape=jax.ShapeDtypeStruct((out_rows // na, na, cw), F32),
        compiler_params=_params("parallel", "parallel", "parallel"),
        name="fnet_dft_cols",
    )(z.reshape(2, batch * na, nb, cw), twc, tws, m_b, cs)
    return out.reshape(out_rows, cw)


def _fft_ctx_kernel(x_ref, yb_hbm_ref, f_ref, cs_ref, o_ref):
    del yb_hbm_ref
    ch = FOURIER_CH
    f = f_ref[...]
    for g in range(x_ref.shape[1] // ch):
        ab = _dot(x_ref[:, g * ch:(g + 1) * ch].astype(BF16), cs_ref[...])
        st = jnp.concatenate([ab[:, :ch], ab[:, ch:]], axis=0).astype(BF16)
        o_ref[:, g * ch:(g + 1) * ch] = _dot(f, st)


def _fourier_context(fz, yb, *, batch, seq, ctx):
    cw = fz.shape[1]
    ch = FOURIER_CH
    cn, sn = _dft_mats(ctx)
    f = (jnp.concatenate([cn, -sn], axis=1) * ((ctx * ch) ** -0.5)).astype(BF16)
    cc, sc = _dft_mats(ch)
    cs = jnp.concatenate([cc, sc], axis=1).astype(BF16)
    blk0 = batch * seq // ctx
    return pl.pallas_call(
        _fft_ctx_kernel,
        grid=(batch,),
        in_specs=[
            pl.BlockSpec((ctx, cw), lambda b: (blk0 + b, 0)),
            pl.BlockSpec(memory_space=pl.ANY),
            _const_spec(f.shape),
            _const_spec(cs.shape),
        ],
        out_specs=pl.BlockSpec((ctx, cw), lambda b: (blk0 + b, 0)),
        out_shape=jax.ShapeDtypeStruct(yb.shape, yb.dtype),
        input_output_aliases={1: 0},
        compiler_params=_params("parallel"),
        name="fnet_context",
    )(fz, yb, f, cs)


def _attn_step(q, k, v, carry):
    m, l, acc = carry
    s = lax.dot_general(q, k, NT, preferred_element_type=F32)
    m_new = jnp.maximum(m, jnp.max(s, axis=-1, keepdims=True))
    alpha = jnp.exp2(m - m_new)
    p = jnp.exp2(s - m_new)
    l = alpha * l + jnp.sum(p, axis=-1, keepdims=True)
    acc = alpha * acc + _dot(p.astype(BF16), v)
    return m_new, l, acc


def _attn_kernel(*refs, kc, latent):
    if latent:
        q_ref, kl_ref, vl_ref, kc_ref, vc_ref, yc_hbm_ref, o_ref = refs
    else:
        q_ref, kc_ref, vc_ref, yc_hbm_ref, o_ref = refs
    del yc_hbm_ref
    tq = q_ref.shape[0]
    sl = [slice(0, MLA_SLOT), slice(MLA_SLOT, 2 * MLA_SLOT)]
    qs = [q_ref[:, s] for s in sl]
    init = (jnp.full((tq, 1), -jnp.inf, F32), jnp.zeros((tq, 1), F32), jnp.zeros((tq, 2 * MLA_V), F32))
    carry = (init, init)
    if latent:
        def body(c, carry):
            rows = pl.ds(pl.multiple_of(c * kc, kc), kc)
            v = vl_ref[rows, :]
            return tuple(_attn_step(qs[h], kl_ref[rows, sl[h]], v, carry[h]) for h in range(2))
        carry = lax.fori_loop(0, kl_ref.shape[0] // kc, body, carry)
    v = vc_ref[...]
    carry = tuple(_attn_step(qs[h], kc_ref[:, sl[h]], v, carry[h]) for h in range(2))
    outs = [acc / l for (_, l, acc) in carry]
    lane = lax.broadcasted_iota(jnp.int32, outs[0].shape, 1)
    o_ref[...] = jnp.where(lane < MLA_V, outs[0], outs[1]).astype(o_ref.dtype)


def _attention(q, k, v, yc, *, latent, batch, seq, ctx, tq, kc):
    hp = MLA_HEADS // 2
    w = 2 * MLA_SLOT
    wv = 2 * MLA_V
    ctx0 = batch * seq // ctx
    kc_spec = pl.BlockSpec((ctx, w), lambda b, h, i: (ctx0 + b, h))
    vc_spec = pl.BlockSpec((ctx, wv), lambda b, h, i: (ctx0 + b, h))
    if latent:
        nq = seq // tq
        qrow = lambda b, i: b * nq + i
        in_specs = [
            pl.BlockSpec((tq, w), lambda b, h, i: (qrow(b, i), h)),
            pl.BlockSpec((seq, w), lambda b, h, i: (b, h)),
            pl.BlockSpec((seq, wv), lambda b, h, i: (b, h)),
            kc_spec, vc_spec,
        ]
        args = [q, k, v, k, v]
    else:
        nq = ctx // tq
        q0 = batch * seq // tq
        qrow = lambda b, i: q0 + b * nq + i
        in_specs = [pl.BlockSpec((tq, w), lambda b, h, i: (qrow(b, i), h)), kc_spec, vc_spec]
        args = [q, k, v]
    n_in = len(args)
    return pl.pallas_call(
        functools.partial(_attn_kernel, kc=kc, latent=latent),
        grid=(batch, hp, nq),
        in_specs=in_specs + [pl.BlockSpec(memory_space=pl.ANY)],
        out_specs=pl.BlockSpec((tq, wv), lambda b, h, i: (qrow(b, i), h)),
        out_shape=jax.ShapeDtypeStruct(yc.shape, yc.dtype),
        input_output_aliases={n_in: 0},
        compiler_params=_params("parallel", "parallel", "arbitrary"),
        name="mla_latent" if latent else "mla_context",
    )(*args, yc)


def _merge_kernel(x_ref, mod_ref, gpre_ref, gpost_ref, wgt_ref, ya_ref, yb_ref, yc_ref, wb_ref, wo_ref, o_ref):
    x = x_ref[...]
    d = x.shape[1]
    h = _modulated(x, gpre_ref[...], mod_ref, 1).astype(BF16)
    ys = (ya_ref[...], yb_ref[...].astype(BF16), yc_ref[...])
    m = jnp.zeros(x.shape, F32)
    for br in range(3):
        gate = jax.nn.sigmoid(_dot(h, wgt_ref[:, br * d:(br + 1) * d]))
        m = m + gate * _dot(ys[br], wb_ref[br])
    out = _dot(m.astype(BF16), wo_ref[...])
    o_ref[...] = x + mod_ref[5:6, :] * _rms(out, gpost_ref[...])


def _merge(x_all, mods, layer, g_pre, g_post, w_gates, ya, yb, yc, wb, wo, *, n_rows, tm, cond_of_tile):
    d = x_all.shape[1]
    bw = ya.shape[1]
    row = lambda w: pl.BlockSpec((tm, w), lambda i: (i, 0))
    return pl.pallas_call(
        _merge_kernel,
        grid=(n_rows // tm,),
        in_specs=[
            row(d),
            pl.BlockSpec((None, None, N_MOD, d), lambda i: (layer, cond_of_tile(i), 0, 0)),
            _const_spec((1, d)),
            _const_spec((1, d)),
            _const_spec(w_gates.shape),
            row(bw), row(bw), row(bw),
            _const_spec(wb.shape),
            _const_spec(wo.shape),
        ],
        out_specs=row(d),
        out_shape=jax.ShapeDtypeStruct((n_rows, d), F32),
        compiler_params=_params("parallel"),
        name="merge",
    )(x_all, mods, g_pre.reshape(1, d), g_post.reshape(1, d), w_gates, ya, yb, yc, wb, wo)


def _rope_table(seq, tm):
    pos = jnp.arange(seq, dtype=jnp.int32)
    rows = (pos // GRID_W).astype(F32)
    cols = (pos % GRID_W).astype(F32)
    half = MLA_ROPE // 2
    inv_freq = ROPE_BASE ** (-jnp.arange(0, half, 2, dtype=F32) / half)
    ar, ac = rows[:, None] * inv_freq, cols[:, None] * inv_freq
    z = jnp.zeros_like(ar)
    cos = jnp.concatenate([jnp.cos(ar), jnp.cos(ar), jnp.cos(ac), jnp.cos(ac)], axis=1)
    s1 = jnp.concatenate([-jnp.sin(ar), z, -jnp.sin(ac), z], axis=1)
    s2 = jnp.concatenate([z, jnp.sin(ar), z, jnp.sin(ac)], axis=1)
    tab = jnp.concatenate([cos, s1, s2, jnp.zeros((seq, LANES - 3 * MLA_ROPE), F32)], axis=1)
    ident = jnp.concatenate([jnp.ones((tm, MLA_ROPE), F32), jnp.zeros((tm, LANES - MLA_ROPE), F32)], axis=1)
    return jnp.concatenate([tab, ident], axis=0)


def _pad_cols(w, n):
    return jnp.pad(w, ((0, 0), (0, n - w.shape[1])))


def _layer_weights(layer, w_in, gla_w_decay, gla_b_decay, mla_w_uq, mla_w_ukv):
    nqk = GLA_HEADS * GLA_DK
    r = GLA_GATE_RANK
    o_dec = 2 * nqk + 2 * GLA_HEADS * GLA_DV
    o_four = o_dec + 2 * r
    o_cq = o_four + FOURIER_GROUPS * FOURIER_CH
    q_rank = mla_w_uq.shape[1]
    kv_rank = mla_w_ukv.shape[1]
    o_ckv = o_cq + q_rank
    o_kr = o_ckv + kv_rank
    o_gates = o_kr + MLA_ROPE
    w = w_in[layer]
    w_in2 = jnp.concatenate([
        w[:, :o_dec], _pad_cols(w[:, o_dec:o_four], LANES), w[:, o_four:o_cq], w[:, o_cq:o_ckv],
        w[:, o_ckv:o_kr], _pad_cols(w[:, o_kr:o_gates], LANES)], axis=1).astype(BF16)
    w_gates = w[:, o_gates:].astype(BF16)
    wd = gla_w_decay[layer]
    wdec = jnp.zeros((LANES, 2 * nqk), F32)
    wdec = wdec.at[0:r, 0:nqk].set(wd[0]).at[r:2 * r, nqk:].set(wd[1]).astype(BF16)
    bdec = gla_b_decay[layer].reshape(1, 2 * nqk)
    dq = MLA_NOPE + MLA_ROPE
    wuq = mla_w_uq[layer].reshape(q_rank, MLA_HEADS, dq)
    wuq = jnp.pad(wuq, ((0, 0), (0, 0), (0, MLA_SLOT - dq))).reshape(q_rank, MLA_HEADS * MLA_SLOT).astype(BF16)
    wkv = mla_w_ukv[layer].reshape(kv_rank, MLA_HEADS, MLA_NOPE + MLA_V)
    wk = jnp.pad(wkv[:, :, :MLA_NOPE], ((0, 0), (0, 0), (0, MLA_SLOT - MLA_NOPE)))
    wukv = jnp.concatenate([wk.reshape(kv_rank, MLA_HEADS * MLA_SLOT),
                            wkv[:, :, MLA_NOPE:].reshape(kv_rank, MLA_HEADS * MLA_V)], axis=1).astype(BF16)
    return w_in2, w_gates, wdec, bdec, wuq, wukv


def kernel(x, c, ctx, c_ctx, w_mod, b_mod, norm_pre, norm_post, ffn_w_gate, ffn_w_up, ffn_w_down, w_in,
           gla_w_decay, gla_b_decay, gla_norm, mla_q_norm, mla_w_uq, mla_kv_norm, mla_w_ukv, w_branch, w_out):
    batch, seq, d = x.shape
    n_ctx = ctx.shape[1]
    depth = w_mod.shape[0]
    n_lat = batch * seq
    t = n_lat + batch * n_ctx
    tm = min(256, n_ctx)
    d_ff = ffn_w_gate.shape[-1]
    fc = d_ff // 2 if (d_ff // 2) % LANES == 0 else d_ff
    assert seq % tm == 0 and n_ctx % tm == 0 and seq % (FFT_NA * 8) == 0 and batch < MOD_ROWS

    cond_of_tile = lambda i: jnp.minimum((i * tm) // seq, batch)
    tab_of_tile = lambda i: jnp.where(i * tm < n_lat, ((i * tm) % seq) // tm, seq // tm)

    cond = jnp.concatenate([c, c_ctx[None, :], jnp.zeros((MOD_ROWS - batch - 1, d), F32)], axis=0)
    mods = _modulation(cond, w_mod, b_mod, tn=(N_MOD * d) // 8).reshape(depth, MOD_ROWS, N_MOD, d)
    tab = _rope_table(seq, tm)
    x_all = jnp.concatenate([x.reshape(n_lat, d), ctx.reshape(batch * n_ctx, d)], axis=0)

    for layer in range(depth):
        last = layer == depth - 1
        ffn = lambda xa, sub, s, rows: _half_ffn(
            xa, mods, layer, sub, norm_pre[layer, sub], norm_post[layer, sub],
            ffn_w_gate[layer, s].astype(BF16), ffn_w_up[layer, s].astype(BF16), ffn_w_down[layer, s].astype(BF16),
            n_rows=rows, tm=tm, cond_of_tile=cond_of_tile, fc=fc)
        x_all = ffn(x_all, 0, 0, t)

        w_in2, w_gates, wdec, bdec, wuq, wukv = _layer_weights(
            layer, w_in, gla_w_decay, gla_b_decay, mla_w_uq, mla_w_ukv)
        gla_in, fz, q, k, v = _in_projection(
            x_all, mods, layer, norm_pre[layer, 1], w_in2, wdec, bdec,
            mla_q_norm[layer].reshape(1, -1), wuq, mla_kv_norm[layer].reshape(1, -1), wukv, tab,
            tm=tm, cond_of_tile=cond_of_tile, tab_of_tile=tab_of_tile)

        scan = functools.partial(_gla_scan, tb=tm, batch=batch, seq=seq, ctx=n_ctx)
        o_fwd = scan(gla_in, None, None, reverse=False)
        ya = scan(gla_in, o_fwd, gla_norm[layer], reverse=True)

        rows_out = n_lat if last else t
        yb = _fourier_latent(fz, rows_out, batch=batch, seq=seq, kab=16, lb=4096)
        yc = jnp.zeros((rows_out, MLA_HEADS * MLA_V), BF16)
        yc = _attention(q, k, v, yc, latent=True, batch=batch, seq=seq, ctx=n_ctx,
                        tq=min(ATTN_TQ, seq), kc=min(ATTN_KC, seq))
        if not last:
            yb = _fourier_context(fz, yb, batch=batch, seq=seq, ctx=n_ctx)
            yc = _attention(q, k, v, yc, latent=False, batch=batch, seq=seq, ctx=n_ctx, tq=tm, kc=256)

        x_all = _merge(x_all, mods, layer, norm_pre[layer, 1], norm_post[layer, 1], w_gates, ya, yb, yc,
                       w_branch[layer].astype(BF16), w_out[layer].astype(BF16),
                       n_rows=rows_out, tm=tm, cond_of_tile=cond_of_tile)
        x_all = ffn(x_all, 2, 1, rows_out)
    return x_all.reshape(batch, seq, d)
```

```python
import functools

import jax
import jax.numpy as jnp
import numpy as np
from jax import lax
from jax.experimental import pallas as pl
from jax.experimental.pallas import tpu as pltpu

F32 = jnp.float32
BF16 = jnp.bfloat16

GRID_W = 64
N_MOD = 9
GLA_HEADS = 4
GLA_DK = 64
GLA_DV = 128
GLA_GATE_RANK = 16
GLA_TAU = 16.0
GLA_CHUNK = 64
GLA_EXP_CLAMP = 80.0
FOURIER_GROUPS = 4
FOURIER_CH = 128
FFT_NA = 64
MLA_HEADS = 8
MLA_NOPE = 64
MLA_ROPE = 32
MLA_V = 64
MLA_SLOT = 128
ROPE_BASE = 10000.0
EPS = 1e-6
LOG2_E = 1.4426950408889634
ATTN_TQ = 512
ATTN_KC = 4096
LANES = 128
MOD_ROWS = 8
VMEM_LIMIT = 56 * 1024 * 1024

NT = (((1,), (1,)), ((), ()))
TN = (((0,), (0,)), ((), ()))


def _params(*sem):
    return pltpu.CompilerParams(dimension_semantics=sem, vmem_limit_bytes=VMEM_LIMIT)


def _const_spec(shape):
    nd = len(shape)
    return pl.BlockSpec(shape, lambda *_: (0,) * nd, pipeline_mode=pl.Buffered(1))


def _dot(a, b):
    return jnp.dot(a, b, preferred_element_type=F32)


def _rms(x, g):
    return x * lax.rsqrt(jnp.mean(x * x, axis=-1, keepdims=True) + EPS) * g


def _silu(x):
    return x * jax.nn.sigmoid(x)


def _modulated(x, g, mod_ref, sub):
    shift = mod_ref[3 * sub + 0:3 * sub + 1, :]
    scale = mod_ref[3 * sub + 1:3 * sub + 2, :]
    return _rms(x, g) * (1.0 + scale) + shift


def _mod_kernel(c_ref, w_ref, b_ref, o_ref):
    a = _silu(c_ref[...]).astype(BF16)
    o_ref[0] = _dot(a, w_ref[0].astype(BF16)) + b_ref[0]


def _modulation(cond, w_mod, b_mod, tn):
    depth, d, nm = w_mod.shape
    return pl.pallas_call(
        _mod_kernel,
        grid=(depth, nm // tn),
        in_specs=[
            pl.BlockSpec((MOD_ROWS, d), lambda l, j: (0, 0)),
            pl.BlockSpec((1, d, tn), lambda l, j: (l, 0, j)),
            pl.BlockSpec((1, 1, tn), lambda l, j: (l, 0, j)),
        ],
        out_specs=pl.BlockSpec((1, MOD_ROWS, tn), lambda l, j: (l, 0, j)),
        out_shape=jax.ShapeDtypeStruct((depth, MOD_ROWS, nm), F32),
        compiler_params=_params("parallel", "parallel"),
        name="modulation",
    )(cond, w_mod, b_mod.reshape(depth, 1, nm))


def _ffn_kernel(*refs, sub, fc, n_lat_tiles):
    if n_lat_tiles is None:
        x_ref, mod_ref, gpre_ref, gpost_ref, wg_ref, wu_ref, wd_ref, o_ref = refs
        x = x_ref[...]
    else:
        x_ref, xc_ref, mod_ref, gpre_ref, gpost_ref, wg_ref, wu_ref, wd_ref, o_ref = refs
        x = jnp.where(pl.program_id(0) >= n_lat_tiles, xc_ref[...], x_ref[...])
    h = _modulated(x, gpre_ref[...], mod_ref, sub).astype(BF16)
    d_ff = wg_ref.shape[1]
    y = jnp.zeros(x.shape, F32)
    for c0 in range(0, d_ff, fc):
        g = _dot(h, wg_ref[:, c0:c0 + fc])
        u = _dot(h, wu_ref[:, c0:c0 + fc])
        a = (_silu(g) * u).astype(BF16)
        y = y + _dot(a, wd_ref[c0:c0 + fc, :])
    gate = mod_ref[3 * sub + 2:3 * sub + 3, :]
    o_ref[...] = x + 0.5 * gate * _rms(y, gpost_ref[...])


def _half_ffn(x_all, x_ctx, mods, layer, sub, g_pre, g_post, wg, wu, wd, *, n_rows, tm, cond_of_tile, fc):
    d = x_all.shape[1]
    d_ff = wg.shape[1]
    if x_ctx is None:
        n_lat_tiles = None
        x_specs, xs = [pl.BlockSpec((tm, d), lambda i: (i, 0))], [x_all]
    else:
        n_lat_tiles = x_all.shape[0] // tm
        x_specs = [pl.BlockSpec((tm, d), lambda i: (jnp.minimum(i, n_lat_tiles - 1), 0)),
                   pl.BlockSpec((tm, d), lambda i: (jnp.maximum(i - n_lat_tiles, 0), 0))]
        xs = [x_all, x_ctx]
    return pl.pallas_call(
        functools.partial(_ffn_kernel, sub=sub, fc=fc, n_lat_tiles=n_lat_tiles),
        grid=(n_rows // tm,),
        in_specs=[
            *x_specs,
            pl.BlockSpec((None, None, N_MOD, d), lambda i: (layer, cond_of_tile(i), 0, 0)),
            _const_spec((1, d)),
            _const_spec((1, d)),
            _const_spec((d, d_ff)),
            _const_spec((d, d_ff)),
            _const_spec((d_ff, d)),
        ],
        out_specs=pl.BlockSpec((tm, d), lambda i: (i, 0)),
        out_shape=jax.ShapeDtypeStruct((n_rows, d), F32),
        compiler_params=_params("parallel"),
        name=f"half_ffn_{sub}",
    )(*xs, mods, g_pre.reshape(1, d), g_post.reshape(1, d), wg, wu, wd)


_Z_GLA = 0
_Z_DEC = 1536
_Z_FOUR = 1664
_Z_CQ = 2176
_Z_CKV = 2560
_Z_KR = 2816
_Z_END = 2944


def _rope_slot(x, tab):
    lane = lax.broadcasted_iota(jnp.int32, x.shape, 1)
    in_rope = (lane >= MLA_NOPE) & (lane < MLA_NOPE + MLA_ROPE)
    c = jnp.where(in_rope, pltpu.roll(tab, 64, 1), 1.0)
    s1 = jnp.where(in_rope, pltpu.roll(tab, 32, 1), 0.0)
    s2 = jnp.where(in_rope, tab, 0.0)
    h = MLA_ROPE // 4
    return x * c + pltpu.roll(x, LANES - h, 1) * s1 + pltpu.roll(x, h, 1) * s2


def _inproj_kernel(x_ref, mod_ref, gpre_ref, win_ref, wdec_ref, bdec_ref, qn_ref, wuq_ref,
                   kvn_ref, wukv_ref, tab_ref, gla_ref, fz_ref, q_ref, k_ref, v_ref):
    h = _modulated(x_ref[...], gpre_ref[...], mod_ref, 1).astype(BF16)
    z = _dot(h, win_ref[...])
    nqk = GLA_HEADS * GLA_DK
    gla_ref[:, 0:nqk] = z[:, 0:nqk] * (GLA_DK ** -0.5)
    gla_ref[:, nqk:_Z_DEC] = z[:, nqk:_Z_DEC]
    xd = _dot(z[:, _Z_DEC:_Z_FOUR].astype(BF16), wdec_ref[...]) + bdec_ref[...]
    logsig = jnp.minimum(xd, 0.0) - jnp.log1p(jnp.exp(-jnp.abs(xd)))
    gla_ref[:, _Z_DEC:_Z_DEC + 2 * nqk] = logsig * (1.0 / GLA_TAU)
    fz_ref[...] = z[:, _Z_FOUR:_Z_CQ]
    tab = tab_ref[...]
    cq = _rms(z[:, _Z_CQ:_Z_CKV], qn_ref[...]).astype(BF16)
    q = _dot(cq, wuq_ref[...])
    ckv = _rms(z[:, _Z_CKV:_Z_KR], kvn_ref[...]).astype(BF16)
    kv = _dot(ckv, wukv_ref[...])
    nk = MLA_HEADS * MLA_SLOT
    kr = _rope_slot(pltpu.roll(z[:, _Z_KR:_Z_END], MLA_NOPE, 1), tab)
    scale = (MLA_NOPE + MLA_ROPE) ** -0.5 * LOG2_E
    for hd in range(MLA_HEADS):
        sl = slice(hd * MLA_SLOT, (hd + 1) * MLA_SLOT)
        q_ref[:, sl] = (_rope_slot(q[:, sl], tab) * scale).astype(BF16)
        k_ref[:, sl] = (kv[:, sl] + kr).astype(BF16)
    lane = lax.broadcasted_iota(jnp.int32, (x_ref.shape[0], nk), 1)
    v_ref[...] = jnp.where(lane % MLA_SLOT == MLA_V, 1.0, kv[:, nk:]).astype(BF16)


def _in_projection(x_all, mods, layer, g_pre, w_in2, wdec, bdec, qn, wuq, kvn, wukv, tab,
                   *, tm, cond_of_tile, tab_of_tile, kv_of_tile):
    t, d = x_all.shape
    nq = MLA_HEADS * MLA_SLOT
    row = lambda w: pl.BlockSpec((tm, w), lambda i: (i, 0))
    kv_row = pl.BlockSpec((tm, nq), lambda i: (kv_of_tile(i), 0))
    gla_row = pl.BlockSpec((tm, 2048), lambda i: (kv_of_tile(i), 0))
    return pl.pallas_call(
        _inproj_kernel,
        grid=(t // tm,),
        in_specs=[
            row(d),
            pl.BlockSpec((None, None, N_MOD, d), lambda i: (layer, cond_of_tile(i), 0, 0)),
            _const_spec((1, d)),
            _const_spec(w_in2.shape),
            _const_spec(wdec.shape),
            _const_spec(bdec.shape),
            _const_spec(qn.shape),
            _const_spec(wuq.shape),
            _const_spec(kvn.shape),
            _const_spec(wukv.shape),
            pl.BlockSpec((tm, LANES), lambda i: (tab_of_tile(i), 0)),
        ],
        out_specs=[gla_row, row(512), row(nq), kv_row, kv_row],
        out_shape=[
            jax.ShapeDtypeStruct((t, 2048), F32),
            jax.ShapeDtypeStruct((t, 512), F32),
            jax.ShapeDtypeStruct((t, nq), BF16),
            jax.ShapeDtypeStruct((t, nq), BF16),
            jax.ShapeDtypeStruct((t, nq), BF16),
        ],
        compiler_params=_params("parallel"),
        name="in_projection",
    )(x_all, mods, g_pre.reshape(1, d), w_in2, wdec, bdec, qn, wuq, kvn, wukv, tab)


def _gla_kernel(*refs, reverse, final):
    if final:
        g_ref, of_ref, gn_ref, o_ref, st_ref = refs
    else:
        g_ref, o_ref, st_ref = refs
    c = GLA_CHUNK
    n_batch, tb = g_ref.shape[0], g_ref.shape[1]
    nqk = GLA_HEADS * GLA_DK

    @pl.when(pl.program_id(0) == 0)
    def _():
        st_ref[...] = jnp.zeros(st_ref.shape, F32)

    row = lax.broadcasted_iota(jnp.int32, (c, c), 0)
    col = lax.broadcasted_iota(jnp.int32, (c, c), 1)
    keep = (col >= row) if reverse else (col <= row)
    lmat = jnp.where(keep, 1.0, 0.0).astype(BF16)
    for bi in range(n_batch):
        _gla_block(g_ref.at[bi], of_ref.at[bi] if final else None, gn_ref if final else None, o_ref.at[bi],
                   st_ref.at[bi], keep, lmat, reverse=reverse, final=final)


def _gla_block(g_ref, of_ref, gn_ref, o_ref, st_ref, keep, lmat, *, reverse, final):
    c = GLA_CHUNK
    tb = g_ref.shape[0]
    nqk = GLA_HEADS * GLA_DK
    la_off = _Z_DEC + (nqk if reverse else 0)
    mid = c // 2 if reverse else c // 2 - 1
    end = 0 if reverse else c - 1
    states = [st_ref[hd] for hd in range(GLA_HEADS)]
    chunks = range(tb // c)
    for ci in (reversed(chunks) if reverse else chunks):
        rows = slice(ci * c, (ci + 1) * c)
        la = g_ref[rows, la_off:la_off + nqk]
        h1 = la.astype(BF16)
        r1 = la - h1.astype(F32)
        h2 = r1.astype(BF16)
        h3 = (r1 - h2.astype(F32)).astype(BF16)
        b = _dot(lmat, h1) + _dot(lmat, h2) + _dot(lmat, h3)
        ref = b[mid:mid + 1, :]
        tot = b[end:end + 1, :]
        qt = g_ref[rows, 0:nqk] * jnp.exp(b - ref)
        kt = g_ref[rows, nqk:2 * nqk] * jnp.exp(jnp.minimum(ref - b, GLA_EXP_CLAMP))
        q_in = (qt * jnp.exp(ref)).astype(BF16)
        k_st = (kt * jnp.exp(tot - ref)).astype(BF16)
        qt = qt.astype(BF16)
        kt = kt.astype(BF16)
        dec = jnp.exp(tot)
        for hd in range(GLA_HEADS):
            ks = slice(hd * GLA_DK, (hd + 1) * GLA_DK)
            vs = slice(hd * GLA_DV, (hd + 1) * GLA_DV)
            v = g_ref[rows, 2 * nqk + hd * GLA_DV:2 * nqk + (hd + 1) * GLA_DV].astype(BF16)
            a = lax.dot_general(qt[:, ks], kt[:, ks], NT, preferred_element_type=F32)
            a = jnp.where(keep, a, 0.0).astype(BF16)
            st = states[hd]
            o = _dot(a, v) + lax.dot_general(q_in[:, ks], st.astype(BF16), NT, preferred_element_type=F32)
            states[hd] = st * dec[:, ks] + lax.dot_general(v, k_st[:, ks], TN, preferred_element_type=F32)
            if final:
                o = o + of_ref[rows, vs]
                gate = g_ref[rows, 2 * nqk + GLA_HEADS * GLA_DV + hd * GLA_DV:
                             2 * nqk + GLA_HEADS * GLA_DV + (hd + 1) * GLA_DV]
                o_ref[rows, vs] = (_rms(o, gn_ref[...]) * _silu(gate)).astype(o_ref.dtype)
            else:
                o_ref[rows, vs] = o
    for hd in range(GLA_HEADS):
        st_ref[hd] = states[hd]


def _gla_scan(gla_in, o_fwd, g_norm, *, reverse, tb, seq, ctx):
    batch = gla_in.shape[0]
    n_lat, n_ctx = seq // tb, ctx // tb

    def blk(j):
        if reverse:
            return jnp.where(j < n_ctx, n_lat + (n_ctx - 1 - j), n_lat + n_ctx - 1 - j)
        return jnp.where(j < n_ctx, n_lat + j, j - n_ctx)

    final = o_fwd is not None
    dv = GLA_HEADS * GLA_DV
    in_specs = [pl.BlockSpec((batch, tb, gla_in.shape[2]), lambda j: (0, blk(j), 0))]
    args = [gla_in]
    if final:
        in_specs += [pl.BlockSpec((batch, tb, dv), lambda j: (0, blk(j), 0)), _const_spec((1, GLA_DV))]
        args += [o_fwd, g_norm.reshape(1, GLA_DV)]
    return pl.pallas_call(
        functools.partial(_gla_kernel, reverse=reverse, final=final),
        grid=(n_lat + n_ctx,),
        in_specs=in_specs,
        out_specs=pl.BlockSpec((batch, tb, dv), lambda j: (0, blk(j), 0)),
        out_shape=jax.ShapeDtypeStruct((batch, seq + ctx, dv), BF16 if final else F32),
        scratch_shapes=[pltpu.VMEM((batch, GLA_HEADS, GLA_DV, GLA_DK), F32)],
        compiler_params=_params("arbitrary"),
        name="gla_bwd_scan" if reverse else "gla_fwd_scan",
    )(*args)


def _dft_mats(n):
    k = np.arange(n)
    ang = (2.0 * np.pi / n) * ((k[:, None] * k[None, :]) % n)
    return np.cos(ang), np.sin(ang)


def _fft_kernel(x_ref, fa_ref, twc_ref, tws_ref, mb_ref, cs_ref, o_ref, zr_ref, zi_ref, *, na, nb, bg, pitch):
    ch = x_ref.shape[1]
    fa = fa_ref[...].astype(BF16)
    for b0 in range(0, nb, bg):
        xs = jnp.concatenate([x_ref[pl.ds(b0 + i, na, stride=nb), :] for i in range(bg)], axis=1)
        z = _dot(fa, xs.astype(BF16))
        for i in range(bg):
            r0 = (b0 + i) * pitch
            zr_ref[r0:r0 + na, :] = z[:na, i * ch:(i + 1) * ch]
            zi_ref[r0:r0 + na, :] = z[na:, i * ch:(i + 1) * ch]
    mb = mb_ref[...].astype(BF16)
    cs = cs_ref[...].astype(BF16)

    for ka in range(na):
        zr, zi = zr_ref[pl.ds(ka, nb, stride=pitch), :], zi_ref[pl.ds(ka, nb, stride=pitch), :]
        c = twc_ref[ka * nb:(ka + 1) * nb, :]
        s = tws_ref[ka * nb:(ka + 1) * nb, :]
        t = jnp.concatenate([zr * c + zi * s, zi * c - zr * s], axis=0).astype(BF16)
        v = _dot(mb, t)
        vv = jnp.concatenate([v[:nb], v[nb:]], axis=1).astype(BF16)
        o_ref[pl.ds(ka, nb, stride=na), :] = _dot(vv, cs)


def _fourier_latent(fz, *, batch, seq):
    cw = fz.shape[1]
    na, nb = FFT_NA, seq // FFT_NA
    pitch = na + 1
    ch = FOURIER_CH
    ca, sa = _dft_mats(na)
    f_a = np.concatenate([ca, -sa], axis=0).astype(np.float32)
    cb, sb = _dft_mats(nb)
    m_b = np.concatenate([np.concatenate([cb, sb], axis=1),
                          np.concatenate([-sb, cb], axis=1)], axis=0).astype(np.float32)
    cc, sc = _dft_mats(ch)
    cs = (np.concatenate([cc, sc], axis=0) * ((seq * ch) ** -0.5)).astype(np.float32)
    ang = (2.0 * np.pi / seq) * ((np.arange(na)[:, None] * np.arange(nb)[None, :]) % seq)
    twc = jnp.broadcast_to(np.cos(ang).reshape(seq, 1).astype(np.float32), (seq, ch))
    tws = jnp.broadcast_to(np.sin(ang).reshape(seq, 1).astype(np.float32), (seq, ch))
    return pl.pallas_call(
        functools.partial(_fft_kernel, na=na, nb=nb, bg=4, pitch=pitch),
        grid=(batch, cw // ch),
        in_specs=[pl.BlockSpec((seq, ch), lambda b, g: (b, g)), _const_spec(f_a.shape),
                  _const_spec((seq, ch)), _const_spec((seq, ch)), _const_spec(m_b.shape), _const_spec(cs.shape)],
        out_specs=pl.BlockSpec((seq, ch), lambda b, g: (b, g)),
        out_shape=jax.ShapeDtypeStruct((batch * seq, cw), F32),
        scratch_shapes=[pltpu.VMEM((nb * pitch, ch), F32), pltpu.VMEM((nb * pitch, ch), F32)],
        compiler_params=_params("parallel", "parallel"),
        name="fnet_latent",
    )(fz, f_a, twc, tws, m_b, cs)


def _fft_ctx_kernel(x_ref, f_ref, cs_ref, o_ref):
    ch = FOURIER_CH
    f = f_ref[...].astype(BF16)
    cs = cs_ref[...].astype(BF16)
    for g in range(x_ref.shape[1] // ch):
        ab = _dot(x_ref[:, g * ch:(g + 1) * ch].astype(BF16), cs)
        st = jnp.concatenate([ab[:, :ch], ab[:, ch:]], axis=0).astype(BF16)
        o_ref[:, g * ch:(g + 1) * ch] = _dot(f, st)


def _fourier_context(fz, *, batch, seq, ctx):
    cw = fz.shape[1]
    ch = FOURIER_CH
    cn, sn = _dft_mats(ctx)
    f = (np.concatenate([cn, -sn], axis=1) * ((ctx * ch) ** -0.5)).astype(np.float32)
    cc, sc = _dft_mats(ch)
    cs = np.concatenate([cc, sc], axis=1).astype(np.float32)
    blk0 = batch * seq // ctx
    return pl.pallas_call(
        _fft_ctx_kernel,
        grid=(batch,),
        in_specs=[
            pl.BlockSpec((ctx, cw), lambda b: (blk0 + b, 0)),
            _const_spec(f.shape),
            _const_spec(cs.shape),
        ],
        out_specs=pl.BlockSpec((ctx, cw), lambda b: (b, 0)),
        out_shape=jax.ShapeDtypeStruct((batch * ctx, cw), F32),
        compiler_params=_params("parallel"),
        name="fnet_context",
    )(fz, f, cs)


def _attn_step(q, k, v, carry):
    m, acc = carry
    s = lax.dot_general(q, k, NT, preferred_element_type=F32)
    m_new = jnp.maximum(m, jnp.max(s, axis=-1, keepdims=True))
    p = jnp.exp2(s - m_new).astype(BF16)
    acc = jnp.exp2(m - m_new) * acc + _dot(p, v)
    return m_new, acc


def _attn_kernel(q_ref, k_ref, v_ref, o_ref, *, kc):
    tq = q_ref.shape[0]
    sl = [slice(0, MLA_SLOT), slice(MLA_SLOT, 2 * MLA_SLOT)]
    qs = [q_ref[:, s] for s in sl]
    init = (jnp.full((tq, 1), -jnp.inf, F32), jnp.zeros((tq, MLA_SLOT), F32))
    carry = (init, init)
    for c0 in range(0, k_ref.shape[0], kc):
        rows = slice(c0, c0 + kc)
        carry = tuple(_attn_step(qs[h], k_ref[rows, sl[h]], v_ref[rows, sl[h]], carry[h]) for h in range(2))
    outs = [acc * (1.0 / acc[:, MLA_V:MLA_V + 1]) for (_, acc) in carry]
    lane = lax.broadcasted_iota(jnp.int32, outs[0].shape, 1)
    o_ref[...] = jnp.where(lane < MLA_V, outs[0], pltpu.roll(outs[1], MLA_V, 1)).astype(o_ref.dtype)


def _attention(q, k, v, *, latent, batch, seq, ctx, tq, kc):
    hp = MLA_HEADS // 2
    w = 2 * MLA_SLOT
    wv = 2 * MLA_V
    if latent:
        nq = seq // tq
        q_spec = pl.BlockSpec((tq, w), lambda b, h, i: (b * nq + i, h))
        kv_spec = pl.BlockSpec((seq + ctx, w), lambda b, h, i: (b, h))
    else:
        nq = ctx // tq
        q0 = batch * seq // tq
        per_batch = (seq + ctx) // ctx
        q_spec = pl.BlockSpec((tq, w), lambda b, h, i: (q0 + b * nq + i, h))
        kv_spec = pl.BlockSpec((ctx, w), lambda b, h, i: (b * per_batch + seq // ctx, h))
    return pl.pallas_call(
        functools.partial(_attn_kernel, kc=kc),
        grid=(batch, hp, nq),
        in_specs=[q_spec, kv_spec, kv_spec],
        out_specs=pl.BlockSpec((tq, wv), lambda b, h, i: (b * nq + i, h)),
        out_shape=jax.ShapeDtypeStruct((batch * nq * tq, hp * wv), BF16),
        compiler_params=_params("parallel", "parallel", "arbitrary"),
        name="mla_latent" if latent else "mla_context",
    )(q, k, v)


def _merge_kernel(*refs, n_lat_tiles, with_ctx):
    if with_ctx:
        (x_ref, mod_ref, gpre_ref, gpost_ref, wgt_ref, ya_ref, yb_ref, yc_ref, ybc_ref, ycc_ref,
         wb_ref, wo_ref, o_ref) = refs
        is_ctx = pl.program_id(0) >= n_lat_tiles
        yb = jnp.where(is_ctx, ybc_ref[...], yb_ref[...])
        yc = jnp.where(is_ctx, ycc_ref[...], yc_ref[...])
    else:
        x_ref, mod_ref, gpre_ref, gpost_ref, wgt_ref, ya_ref, yb_ref, yc_ref, wb_ref, wo_ref, o_ref = refs
        yb, yc = yb_ref[...], yc_ref[...]
    x = x_ref[...]
    d = x.shape[1]
    h = _modulated(x, gpre_ref[...], mod_ref, 1).astype(BF16)
    ys = (ya_ref[...], yb.astype(BF16), yc)
    m = jnp.zeros(x.shape, F32)
    for br in range(3):
        gate = jax.nn.sigmoid(_dot(h, wgt_ref[:, br * d:(br + 1) * d]))
        m = m + gate * _dot(ys[br], wb_ref[br])
    out = _dot(m.astype(BF16), wo_ref[...])
    o_ref[...] = x + mod_ref[5:6, :] * _rms(out, gpost_ref[...])


def _merge(x_all, mods, layer, g_pre, g_post, w_gates, ya, yb, yc, yb_ctx, yc_ctx, wb, wo,
           *, n_rows, n_lat, tm, cond_of_tile, kv_of_tile):
    d = x_all.shape[1]
    bw = ya.shape[1]
    with_ctx = yb_ctx is not None
    n_lat_tiles = n_lat // tm
    row = lambda w: pl.BlockSpec((tm, w), lambda i: (i, 0))
    lat = lambda w: pl.BlockSpec((tm, w), lambda i: (jnp.minimum(i, n_lat_tiles - 1), 0))
    ctx = lambda w: pl.BlockSpec((tm, w), lambda i: (jnp.maximum(i - n_lat_tiles, 0), 0))
    ya_spec = pl.BlockSpec((tm, bw), lambda i: (kv_of_tile(i), 0))
    branch_specs = [ya_spec, lat(bw), lat(bw)] + ([ctx(bw), ctx(bw)] if with_ctx else [])
    branch_args = [ya, yb, yc] + ([yb_ctx, yc_ctx] if with_ctx else [])
    return pl.pallas_call(
        functools.partial(_merge_kernel, n_lat_tiles=n_lat_tiles, with_ctx=with_ctx),
        grid=(n_rows // tm,),
        in_specs=[
            row(d),
            pl.BlockSpec((None, None, N_MOD, d), lambda i: (layer, cond_of_tile(i), 0, 0)),
            _const_spec((1, d)),
            _const_spec((1, d)),
            _const_spec(w_gates.shape),
            *branch_specs,
            _const_spec(wb.shape),
            _const_spec(wo.shape),
        ],
        out_specs=row(d),
        out_shape=jax.ShapeDtypeStruct((n_rows, d), F32),
        compiler_params=_params("parallel"),
        name="merge",
    )(x_all, mods, g_pre.reshape(1, d), g_post.reshape(1, d), w_gates, *branch_args, wb, wo)


def _rope_table(seq, tm):
    pos = np.arange(seq)
    half = MLA_ROPE // 2
    inv_freq = ROPE_BASE ** (-np.arange(0, half, 2) / half)
    ar, ac = (pos // GRID_W)[:, None] * inv_freq, (pos % GRID_W)[:, None] * inv_freq
    z = np.zeros_like(ar)
    cos = np.concatenate([np.cos(ar), np.cos(ar), np.cos(ac), np.cos(ac)], axis=1)
    s1 = np.concatenate([-np.sin(ar), z, -np.sin(ac), z], axis=1)
    s2 = np.concatenate([z, np.sin(ar), z, np.sin(ac)], axis=1)
    tab = np.concatenate([cos, s1, s2, np.zeros((seq, LANES - 3 * MLA_ROPE))], axis=1)
    ident = np.concatenate([np.ones((tm, MLA_ROPE)), np.zeros((tm, LANES - MLA_ROPE))], axis=1)
    return np.concatenate([tab, ident], axis=0).astype(np.float32)


def _pad_cols(w, n):
    return jnp.pad(w, ((0, 0), (0, n - w.shape[1])))


def _layer_weights(layer, w_in, gla_w_decay, gla_b_decay, mla_w_uq, mla_w_ukv):
    nqk = GLA_HEADS * GLA_DK
    r = GLA_GATE_RANK
    o_dec = 2 * nqk + 2 * GLA_HEADS * GLA_DV
    o_four = o_dec + 2 * r
    o_cq = o_four + FOURIER_GROUPS * FOURIER_CH
    q_rank = mla_w_uq.shape[1]
    kv_rank = mla_w_ukv.shape[1]
    o_ckv = o_cq + q_rank
    o_kr = o_ckv + kv_rank
    o_gates = o_kr + MLA_ROPE
    w = w_in[layer]
    w_in2 = jnp.concatenate([
        w[:, :o_dec], _pad_cols(w[:, o_dec:o_four], LANES), w[:, o_four:o_cq], w[:, o_cq:o_ckv],
        w[:, o_ckv:o_kr], _pad_cols(w[:, o_kr:o_gates], LANES)], axis=1).astype(BF16)
    w_gates = w[:, o_gates:].astype(BF16)
    wd = gla_w_decay[layer]
    wdec = jnp.zeros((LANES, 2 * nqk), F32)
    wdec = wdec.at[0:r, 0:nqk].set(wd[0]).at[r:2 * r, nqk:].set(wd[1]).astype(BF16)
    bdec = gla_b_decay[layer].reshape(1, 2 * nqk)
    dq = MLA_NOPE + MLA_ROPE
    wuq = mla_w_uq[layer].reshape(q_rank, MLA_HEADS, dq)
    wuq = jnp.pad(wuq, ((0, 0), (0, 0), (0, MLA_SLOT - dq))).reshape(q_rank, MLA_HEADS * MLA_SLOT).astype(BF16)
    wkv = mla_w_ukv[layer].reshape(kv_rank, MLA_HEADS, MLA_NOPE + MLA_V)
    wk = jnp.pad(wkv[:, :, :MLA_NOPE], ((0, 0), (0, 0), (0, MLA_SLOT - MLA_NOPE)))
    wv = jnp.pad(wkv[:, :, MLA_NOPE:], ((0, 0), (0, 0), (0, MLA_SLOT - MLA_V)))
    wukv = jnp.concatenate([wk.reshape(kv_rank, MLA_HEADS * MLA_SLOT),
                            wv.reshape(kv_rank, MLA_HEADS * MLA_SLOT)], axis=1).astype(BF16)
    return w_in2, w_gates, wdec, bdec, wuq, wukv


def kernel(x, c, ctx, c_ctx, w_mod, b_mod, norm_pre, norm_post, ffn_w_gate, ffn_w_up, ffn_w_down, w_in,
           gla_w_decay, gla_b_decay, gla_norm, mla_q_norm, mla_w_uq, mla_kv_norm, mla_w_ukv, w_branch, w_out):
    batch, seq, d = x.shape
    n_ctx = ctx.shape[1]
    depth = w_mod.shape[0]
    n_lat = batch * seq
    t = n_lat + batch * n_ctx
    tm = min(256, n_ctx)
    d_ff = ffn_w_gate.shape[-1]
    fc = d_ff // 2 if (d_ff // 2) % LANES == 0 else d_ff
    assert seq % tm == 0 and n_ctx % tm == 0 and seq % (FFT_NA * 8) == 0 and batch < MOD_ROWS

    cond_of_tile = lambda i: jnp.minimum((i * tm) // seq, batch)
    tab_of_tile = lambda i: jnp.where(i * tm < n_lat, ((i * tm) % seq) // tm, seq // tm)
    lat_tiles, ctx_tiles = seq // tm, n_ctx // tm

    def kv_of_tile(i):
        j = i - batch * lat_tiles
        lat = (i // lat_tiles) * (lat_tiles + ctx_tiles) + i % lat_tiles
        ctx_row = (j // ctx_tiles) * (lat_tiles + ctx_tiles) + lat_tiles + j % ctx_tiles
        return jnp.where(j < 0, lat, ctx_row)

    n_keys = seq + n_ctx
    key_chunk = max(kc for kc in range(LANES, min(ATTN_KC, n_keys) + 1, LANES) if n_keys % kc == 0)

    cond = jnp.concatenate([c, c_ctx[None, :], jnp.zeros((MOD_ROWS - batch - 1, d), F32)], axis=0)
    mods = _modulation(cond, w_mod, b_mod, tn=(N_MOD * d) // 8).reshape(depth, MOD_ROWS, N_MOD, d)
    tab = _rope_table(seq, tm)
    x_all, x_ctx = x.reshape(n_lat, d), ctx.reshape(batch * n_ctx, d)

    for layer in range(depth):
        last = layer == depth - 1
        ffn = lambda xa, xc, sub, s, rows: _half_ffn(
            xa, xc, mods, layer, sub, norm_pre[layer, sub], norm_post[layer, sub],
            ffn_w_gate[layer, s].astype(BF16), ffn_w_up[layer, s].astype(BF16), ffn_w_down[layer, s].astype(BF16),
            n_rows=rows, tm=tm, cond_of_tile=cond_of_tile, fc=fc)
        x_all = ffn(x_all, x_ctx if layer == 0 else None, 0, 0, t)

        w_in2, w_gates, wdec, bdec, wuq, wukv = _layer_weights(
            layer, w_in, gla_w_decay, gla_b_decay, mla_w_uq, mla_w_ukv)
        gla_in, fz, q, k, v = _in_projection(
            x_all, mods, layer, norm_pre[layer, 1], w_in2, wdec, bdec,
            mla_q_norm[layer].reshape(1, -1), wuq, mla_kv_norm[layer].reshape(1, -1), wukv, tab,
            tm=tm, cond_of_tile=cond_of_tile, tab_of_tile=tab_of_tile, kv_of_tile=kv_of_tile)

        scan = functools.partial(_gla_scan, tb=tm, seq=seq, ctx=n_ctx)
        gla_in = gla_in.reshape(batch, n_keys, gla_in.shape[1])
        o_fwd = scan(gla_in, None, None, reverse=False)
        ya = scan(gla_in, o_fwd, gla_norm[layer], reverse=True).reshape(batch * n_keys, -1)

        rows_out = n_lat if last else t
        yb = _fourier_latent(fz, batch=batch, seq=seq)
        yc = _attention(q, k, v, latent=True, batch=batch, seq=seq, ctx=n_ctx,
                        tq=min(ATTN_TQ, seq), kc=key_chunk)
        yb_ctx = yc_ctx = None
        if not last:
            yb_ctx = _fourier_context(fz, batch=batch, seq=seq, ctx=n_ctx)
            yc_ctx = _attention(q, k, v, latent=False, batch=batch, seq=seq, ctx=n_ctx, tq=tm, kc=n_ctx)

        x_all = _merge(x_all, mods, layer, norm_pre[layer, 1], norm_post[layer, 1], w_gates, ya, yb, yc,
                       yb_ctx, yc_ctx, w_branch[layer].astype(BF16), w_out[layer].astype(BF16),
                       n_rows=rows_out, n_lat=n_lat, tm=tm, cond_of_tile=cond_of_tile, kv_of_tile=kv_of_tile)
        x_all = ffn(x_all, None, 2, 1, rows_out)
    return x_all.reshape(batch, seq, d)
```

```python
import functools

import jax
import jax.numpy as jnp
import numpy as np
from jax import lax
from jax.experimental import pallas as pl
from jax.experimental.pallas import tpu as pltpu

F32 = jnp.float32
BF16 = jnp.bfloat16

GRID_W = 64
N_MOD = 9
GLA_HEADS = 4
GLA_DK = 64
GLA_DV = 128
GLA_GATE_RANK = 16
GLA_TAU = 16.0
GLA_CHUNK = 64
GLA_EXP_CLAMP = 80.0
FOURIER_GROUPS = 4
FOURIER_CH = 128
FFT_NA = 64
MLA_HEADS = 8
MLA_NOPE = 64
MLA_ROPE = 32
MLA_V = 64
MLA_SLOT = 128
ROPE_BASE = 10000.0
EPS = 1e-6
LOG2_E = 1.4426950408889634
FFN_TM = 512
ATTN_TQ = 512
ATTN_KC = 4096
LANES = 128
MOD_ROWS = 8
VMEM_LIMIT = 56 * 1024 * 1024

NT = (((1,), (1,)), ((), ()))
TN = (((0,), (0,)), ((), ()))


def _params(*sem):
    return pltpu.CompilerParams(dimension_semantics=sem, vmem_limit_bytes=VMEM_LIMIT)


def _const_spec(shape):
    nd = len(shape)
    return pl.BlockSpec(shape, lambda *_: (0,) * nd, pipeline_mode=pl.Buffered(1))


def _dot(a, b):
    return jnp.dot(a, b, preferred_element_type=F32)


def _rms(x, g):
    return x * lax.rsqrt(jnp.mean(x * x, axis=-1, keepdims=True) + EPS) * g


def _silu(x):
    return x * jax.nn.sigmoid(x)


def _modulated(x, g, mod_ref, sub):
    shift = mod_ref[3 * sub + 0:3 * sub + 1, :]
    scale = mod_ref[3 * sub + 1:3 * sub + 2, :]
    return _rms(x, g) * (1.0 + scale) + shift


def _mod_kernel(c_ref, w_ref, b_ref, o_ref):
    a = _silu(c_ref[...]).astype(BF16)
    o_ref[0] = _dot(a, w_ref[0].astype(BF16)) + b_ref[0]


def _modulation(cond, w_mod, b_mod, tn):
    depth, d, nm = w_mod.shape
    return pl.pallas_call(
        _mod_kernel,
        grid=(depth, nm // tn),
        in_specs=[
            pl.BlockSpec((MOD_ROWS, d), lambda l, j: (0, 0)),
            pl.BlockSpec((1, d, tn), lambda l, j: (l, 0, j)),
            pl.BlockSpec((1, 1, tn), lambda l, j: (l, 0, j)),
        ],
        out_specs=pl.BlockSpec((1, MOD_ROWS, tn), lambda l, j: (l, 0, j)),
        out_shape=jax.ShapeDtypeStruct((depth, MOD_ROWS, nm), F32),
        compiler_params=_params("parallel", "parallel"),
        name="modulation",
    )(cond, w_mod, b_mod.reshape(depth, 1, nm))


def _ffn_kernel(*refs, sub, fc, n_lat_tiles):
    if n_lat_tiles is None:
        x_ref, mod_ref, gpre_ref, gpost_ref, wg_ref, wu_ref, wd_ref, o_ref = refs
        xc_ref = None
    else:
        x_ref, xc_ref, mod_ref, gpre_ref, gpost_ref, wg_ref, wu_ref, wd_ref, o_ref = refs
    d_ff = wg_ref.shape[1]
    gate = mod_ref[3 * sub + 2:3 * sub + 3, :]
    tm = x_ref.shape[0]
    half = tm // 2 if tm >= 512 else tm
    for r0 in range(0, tm, half):
        rows = slice(r0, r0 + half)
        x = x_ref[rows, :]
        if xc_ref is not None:
            x = jnp.where(pl.program_id(0) >= n_lat_tiles, xc_ref[rows, :], x)
        h = _modulated(x, gpre_ref[...], mod_ref, sub).astype(BF16)
        y = jnp.zeros(x.shape, F32)
        for c0 in range(0, d_ff, fc):
            g = _dot(h, wg_ref[:, c0:c0 + fc])
            u = _dot(h, wu_ref[:, c0:c0 + fc])
            a = (_silu(g) * u).astype(BF16)
            y = y + _dot(a, wd_ref[c0:c0 + fc, :])
        o_ref[rows, :] = x + 0.5 * gate * _rms(y, gpost_ref[...])


def _half_ffn(x_all, x_ctx, mods, layer, sub, ffn_idx, g_pre, g_post, wg, wu, wd,
              *, n_rows, tm, cond_of_tile, fc):
    d = x_all.shape[1]
    d_ff = wg.shape[-1]
    stacked = lambda r, c: pl.BlockSpec((None, None, r, c), lambda i: (layer, ffn_idx, 0, 0),
                                        pipeline_mode=pl.Buffered(1))
    if x_ctx is None:
        n_lat_tiles = None
        x_specs, xs = [pl.BlockSpec((tm, d), lambda i: (i, 0))], [x_all]
    else:
        n_lat_tiles = x_all.shape[0] // tm
        x_specs = [pl.BlockSpec((tm, d), lambda i: (jnp.minimum(i, n_lat_tiles - 1), 0)),
                   pl.BlockSpec((tm, d), lambda i: (jnp.maximum(i - n_lat_tiles, 0), 0))]
        xs = [x_all, x_ctx]
    return pl.pallas_call(
        functools.partial(_ffn_kernel, sub=sub, fc=fc, n_lat_tiles=n_lat_tiles),
        grid=(n_rows // tm,),
        in_specs=[
            *x_specs,
            pl.BlockSpec((None, None, N_MOD, d), lambda i: (layer, cond_of_tile(i), 0, 0)),
            _const_spec((1, d)),
            _const_spec((1, d)),
            stacked(d, d_ff),
            stacked(d, d_ff),
            stacked(d_ff, d),
        ],
        out_specs=pl.BlockSpec((tm, d), lambda i: (i, 0)),
        out_shape=jax.ShapeDtypeStruct((n_rows, d), F32),
        compiler_params=_params("parallel"),
        name=f"half_ffn_{sub}",
    )(*xs, mods, g_pre.reshape(1, d), g_post.reshape(1, d), wg, wu, wd)


_Z_GLA = 0
_Z_DEC = 1536
_Z_FOUR = 1664
_Z_CQ = 2176
_Z_CKV = 2560
_Z_KR = 2816
_Z_END = 2944


def _rope_slot(x, tab):
    return x * tab[:, :LANES] + pltpu.roll(x, LANES - MLA_ROPE, 1) * tab[:, LANES:]


def _inproj_kernel(x_ref, mod_ref, gpre_ref, win_ref, wdec_ref, bdec_ref, qn_ref, wuq_ref,
                   kvn_ref, wukv_ref, tab_ref, gla_ref, fz_ref, q_ref, k_ref, v_ref):
    h = _modulated(x_ref[...], gpre_ref[...], mod_ref, 1).astype(BF16)
    z = _dot(h, win_ref[...])
    nqk = GLA_HEADS * GLA_DK
    gla_ref[:, 0:nqk] = z[:, 0:nqk] * (GLA_DK ** -0.5)
    gla_ref[:, nqk:_Z_DEC] = z[:, nqk:_Z_DEC]
    xd = _dot(z[:, _Z_DEC:_Z_FOUR].astype(BF16), wdec_ref[...]) + bdec_ref[...]
    logsig = jnp.minimum(xd, 0.0) - jnp.log1p(jnp.exp(-jnp.abs(xd)))
    gla_ref[:, _Z_DEC:_Z_DEC + 2 * nqk] = logsig * (1.0 / GLA_TAU)
    fz_ref[...] = z[:, _Z_FOUR:_Z_CQ]
    tab = tab_ref[...]
    cq = _rms(z[:, _Z_CQ:_Z_CKV], qn_ref[...]).astype(BF16)
    q = _dot(cq, wuq_ref[...])
    ckv = _rms(z[:, _Z_CKV:_Z_KR], kvn_ref[...]).astype(BF16)
    kv = _dot(ckv, wukv_ref[...])
    nk = MLA_HEADS * MLA_SLOT
    kr = _rope_slot(pltpu.roll(z[:, _Z_KR:_Z_END], MLA_NOPE, 1), tab)
    scale = (MLA_NOPE + MLA_ROPE) ** -0.5 * LOG2_E
    for hd in range(MLA_HEADS):
        sl = slice(hd * MLA_SLOT, (hd + 1) * MLA_SLOT)
        q_ref[:, sl] = (_rope_slot(q[:, sl], tab) * scale).astype(BF16)
        k_ref[:, sl] = (kv[:, sl] + kr).astype(BF16)
    lane = lax.broadcasted_iota(jnp.int32, (x_ref.shape[0], nk), 1)
    v_ref[...] = jnp.where(lane % MLA_SLOT == MLA_V, 1.0, kv[:, nk:]).astype(BF16)


def _in_projection(x_all, mods, layer, g_pre, w_in2, wdec, bdec, qn, wuq, kvn, wukv, tab,
                   *, tm, cond_of_tile, tab_of_tile, kv_of_tile):
    t, d = x_all.shape
    nq = MLA_HEADS * MLA_SLOT
    row = lambda w: pl.BlockSpec((tm, w), lambda i: (i, 0))
    kv_row = pl.BlockSpec((tm, nq), lambda i: (kv_of_tile(i), 0))
    gla_row = pl.BlockSpec((tm, 2048), lambda i: (kv_of_tile(i), 0))
    return pl.pallas_call(
        _inproj_kernel,
        grid=(t // tm,),
        in_specs=[
            row(d),
            pl.BlockSpec((None, None, N_MOD, d), lambda i: (layer, cond_of_tile(i), 0, 0)),
            _const_spec((1, d)),
            _const_spec(w_in2.shape),
            _const_spec(wdec.shape),
            _const_spec(bdec.shape),
            _const_spec(qn.shape),
            _const_spec(wuq.shape),
            _const_spec(kvn.shape),
            _const_spec(wukv.shape),
            pl.BlockSpec((tm, 2 * LANES), lambda i: (tab_of_tile(i), 0)),
        ],
        out_specs=[gla_row, row(512), row(nq), kv_row, kv_row],
        out_shape=[
            jax.ShapeDtypeStruct((t, 2048), F32),
            jax.ShapeDtypeStruct((t, 512), F32),
            jax.ShapeDtypeStruct((t, nq), BF16),
            jax.ShapeDtypeStruct((t, nq), BF16),
            jax.ShapeDtypeStruct((t, nq), BF16),
        ],
        compiler_params=_params("parallel"),
        name="in_projection",
    )(x_all, mods, g_pre.reshape(1, d), w_in2, wdec, bdec, qn, wuq, kvn, wukv, tab)


def _gla_kernel(*refs, reverse, final):
    if final:
        g_ref, of_ref, gn_ref, o_ref, st_ref = refs
    else:
        g_ref, o_ref, st_ref = refs
    c = GLA_CHUNK
    n_batch, tb = g_ref.shape[0], g_ref.shape[1]
    nqk = GLA_HEADS * GLA_DK

    @pl.when(pl.program_id(0) == 0)
    def _():
        st_ref[...] = jnp.zeros(st_ref.shape, F32)

    row = lax.broadcasted_iota(jnp.int32, (c, c), 0)
    col = lax.broadcasted_iota(jnp.int32, (c, c), 1)
    keep = (col >= row) if reverse else (col <= row)
    lmat = jnp.where(keep, 1.0, 0.0).astype(BF16)
    for bi in range(n_batch):
        _gla_block(g_ref.at[bi], of_ref.at[bi] if final else None, gn_ref if final else None, o_ref.at[bi],
                   st_ref.at[bi], keep, lmat, reverse=reverse, final=final)


def _gla_block(g_ref, of_ref, gn_ref, o_ref, st_ref, keep, lmat, *, reverse, final):
    c = GLA_CHUNK
    tb = g_ref.shape[0]
    nqk = GLA_HEADS * GLA_DK
    la_off = _Z_DEC + (nqk if reverse else 0)
    mid = c // 2 if reverse else c // 2 - 1
    end = 0 if reverse else c - 1
    states = [st_ref[hd] for hd in range(GLA_HEADS)]
    chunks = range(tb // c)
    for ci in (reversed(chunks) if reverse else chunks):
        rows = slice(ci * c, (ci + 1) * c)
        la = g_ref[rows, la_off:la_off + nqk]
        h1 = la.astype(BF16)
        r1 = la - h1.astype(F32)
        h2 = r1.astype(BF16)
        h3 = (r1 - h2.astype(F32)).astype(BF16)
        b = _dot(lmat, h1) + _dot(lmat, h2) + _dot(lmat, h3)
        ref = b[mid:mid + 1, :]
        tot = b[end:end + 1, :]
        qt = g_ref[rows, 0:nqk] * jnp.exp(b - ref)
        kt = g_ref[rows, nqk:2 * nqk] * jnp.exp(jnp.minimum(ref - b, GLA_EXP_CLAMP))
        q_in = (qt * jnp.exp(ref)).astype(BF16)
        k_st = (kt * jnp.exp(tot - ref)).astype(BF16)
        qt = qt.astype(BF16)
        kt = kt.astype(BF16)
        dec = jnp.exp(tot)
        for hd in range(GLA_HEADS):
            ks = slice(hd * GLA_DK, (hd + 1) * GLA_DK)
            vs = slice(hd * GLA_DV, (hd + 1) * GLA_DV)
            v = g_ref[rows, 2 * nqk + hd * GLA_DV:2 * nqk + (hd + 1) * GLA_DV].astype(BF16)
            a = lax.dot_general(qt[:, ks], kt[:, ks], NT, preferred_element_type=F32)
            a = jnp.where(keep, a, 0.0).astype(BF16)
            st = states[hd]
            o = _dot(a, v) + lax.dot_general(q_in[:, ks], st.astype(BF16), NT, preferred_element_type=F32)
            states[hd] = st * dec[:, ks] + lax.dot_general(v, k_st[:, ks], TN, preferred_element_type=F32)
            if final:
                o = o + of_ref[rows, vs]
                gate = g_ref[rows, 2 * nqk + GLA_HEADS * GLA_DV + hd * GLA_DV:
                             2 * nqk + GLA_HEADS * GLA_DV + (hd + 1) * GLA_DV]
                o_ref[rows, vs] = (_rms(o, gn_ref[...]) * _silu(gate)).astype(o_ref.dtype)
            else:
                o_ref[rows, vs] = o
    for hd in range(GLA_HEADS):
        st_ref[hd] = states[hd]


def _gla_scan(gla_in, o_fwd, g_norm, *, reverse, tb, seq, ctx):
    batch = gla_in.shape[0]
    n_lat, n_ctx = seq // tb, ctx // tb

    def blk(j):
        if reverse:
            return jnp.where(j < n_ctx, n_lat + (n_ctx - 1 - j), n_lat + n_ctx - 1 - j)
        return jnp.where(j < n_ctx, n_lat + j, j - n_ctx)

    final = o_fwd is not None
    dv = GLA_HEADS * GLA_DV
    in_specs = [pl.BlockSpec((batch, tb, gla_in.shape[2]), lambda j: (0, blk(j), 0))]
    args = [gla_in]
    if final:
        in_specs += [pl.BlockSpec((batch, tb, dv), lambda j: (0, blk(j), 0)), _const_spec((1, GLA_DV))]
        args += [o_fwd, g_norm.reshape(1, GLA_DV)]
    return pl.pallas_call(
        functools.partial(_gla_kernel, reverse=reverse, final=final),
        grid=(n_lat + n_ctx,),
        in_specs=in_specs,
        out_specs=pl.BlockSpec((batch, tb, dv), lambda j: (0, blk(j), 0)),
        out_shape=jax.ShapeDtypeStruct((batch, seq + ctx, dv), BF16 if final else F32),
        scratch_shapes=[pltpu.VMEM((batch, GLA_HEADS, GLA_DV, GLA_DK), F32)],
        compiler_params=_params("arbitrary"),
        name="gla_bwd_scan" if reverse else "gla_fwd_scan",
    )(*args)


def _dft_mats(n):
    k = np.arange(n)
    ang = (2.0 * np.pi / n) * ((k[:, None] * k[None, :]) % n)
    return np.cos(ang), np.sin(ang)


def _fft_kernel(x_ref, fa_ref, twc_ref, tws_ref, mb_ref, cs_ref, o_ref, zr_ref, zi_ref, *, na, nb, bg, pitch):
    ch = x_ref.shape[1]
    fa = fa_ref[...].astype(BF16)
    for b0 in range(0, nb, bg):
        xs = jnp.concatenate([x_ref[pl.ds(b0 + i, na, stride=nb), :] for i in range(bg)], axis=1)
        z = _dot(fa, xs.astype(BF16))
        for i in range(bg):
            r0 = (b0 + i) * pitch
            zr_ref[r0:r0 + na, :] = z[:na, i * ch:(i + 1) * ch]
            zi_ref[r0:r0 + na, :] = z[na:, i * ch:(i + 1) * ch]
    mb = mb_ref[...].astype(BF16)
    cs = cs_ref[...].astype(BF16)

    for ka in range(na):
        zr, zi = zr_ref[pl.ds(ka, nb, stride=pitch), :], zi_ref[pl.ds(ka, nb, stride=pitch), :]
        c = twc_ref[ka * nb:(ka + 1) * nb, :]
        s = tws_ref[ka * nb:(ka + 1) * nb, :]
        t = jnp.concatenate([zr * c + zi * s, zi * c - zr * s], axis=0).astype(BF16)
        v = _dot(mb, t)
        vv = jnp.concatenate([v[:nb], v[nb:]], axis=1).astype(BF16)
        o_ref[pl.ds(ka, nb, stride=na), :] = _dot(vv, cs)


def _fourier_latent(fz, *, batch, seq):
    cw = fz.shape[1]
    na, nb = FFT_NA, seq // FFT_NA
    pitch = na + 1
    ch = FOURIER_CH
    ca, sa = _dft_mats(na)
    f_a = np.concatenate([ca, -sa], axis=0).astype(np.float32)
    cb, sb = _dft_mats(nb)
    m_b = np.concatenate([np.concatenate([cb, sb], axis=1),
                          np.concatenate([-sb, cb], axis=1)], axis=0).astype(np.float32)
    cc, sc = _dft_mats(ch)
    cs = (np.concatenate([cc, sc], axis=0) * ((seq * ch) ** -0.5)).astype(np.float32)
    ang = (2.0 * np.pi / seq) * ((np.arange(na)[:, None] * np.arange(nb)[None, :]) % seq)
    twc = jnp.broadcast_to(np.cos(ang).reshape(seq, 1).astype(np.float32), (seq, ch))
    tws = jnp.broadcast_to(np.sin(ang).reshape(seq, 1).astype(np.float32), (seq, ch))
    return pl.pallas_call(
        functools.partial(_fft_kernel, na=na, nb=nb, bg=4, pitch=pitch),
        grid=(batch, cw // ch),
        in_specs=[pl.BlockSpec((seq, ch), lambda b, g: (b, g)), _const_spec(f_a.shape),
                  _const_spec((seq, ch)), _const_spec((seq, ch)), _const_spec(m_b.shape), _const_spec(cs.shape)],
        out_specs=pl.BlockSpec((seq, ch), lambda b, g: (b, g)),
        out_shape=jax.ShapeDtypeStruct((batch * seq, cw), F32),
        scratch_shapes=[pltpu.VMEM((nb * pitch, ch), F32), pltpu.VMEM((nb * pitch, ch), F32)],
        compiler_params=_params("parallel", "parallel"),
        name="fnet_latent",
    )(fz, f_a, twc, tws, m_b, cs)


def _fft_ctx_kernel(x_ref, f_ref, cs_ref, o_ref):
    ch = FOURIER_CH
    f = f_ref[...].astype(BF16)
    cs = cs_ref[...].astype(BF16)
    for g in range(x_ref.shape[1] // ch):
        ab = _dot(x_ref[:, g * ch:(g + 1) * ch].astype(BF16), cs)
        st = jnp.concatenate([ab[:, :ch], ab[:, ch:]], axis=0).astype(BF16)
        o_ref[:, g * ch:(g + 1) * ch] = _dot(f, st)


def _fourier_context(fz, *, batch, seq, ctx):
    cw = fz.shape[1]
    ch = FOURIER_CH
    cn, sn = _dft_mats(ctx)
    f = (np.concatenate([cn, -sn], axis=1) * ((ctx * ch) ** -0.5)).astype(np.float32)
    cc, sc = _dft_mats(ch)
    cs = np.concatenate([cc, sc], axis=1).astype(np.float32)
    blk0 = batch * seq // ctx
    return pl.pallas_call(
        _fft_ctx_kernel,
        grid=(batch,),
        in_specs=[
            pl.BlockSpec((ctx, cw), lambda b: (blk0 + b, 0)),
            _const_spec(f.shape),
            _const_spec(cs.shape),
        ],
        out_specs=pl.BlockSpec((ctx, cw), lambda b: (b, 0)),
        out_shape=jax.ShapeDtypeStruct((batch * ctx, cw), F32),
        compiler_params=_params("parallel"),
        name="fnet_context",
    )(fz, f, cs)


def _attn_step(q, k, v, carry):
    m, acc = carry
    s = lax.dot_general(q, k, NT, preferred_element_type=F32)
    m_new = jnp.maximum(m, jnp.max(s, axis=-1, keepdims=True))
    p = jnp.exp2(s - m_new).astype(BF16)
    acc = jnp.exp2(m - m_new) * acc + _dot(p, v)
    return m_new, acc


def _attn_kernel(q_ref, k_ref, v_ref, o_ref, *, kc):
    tq = q_ref.shape[0]
    sl = [slice(0, MLA_SLOT), slice(MLA_SLOT, 2 * MLA_SLOT)]
    qs = [q_ref[:, s] for s in sl]
    init = (jnp.full((tq, 1), -jnp.inf, F32), jnp.zeros((tq, MLA_SLOT), F32))
    carry = (init, init)
    for c0 in range(0, k_ref.shape[0], kc):
        rows = slice(c0, c0 + kc)
        carry = tuple(_attn_step(qs[h], k_ref[rows, sl[h]], v_ref[rows, sl[h]], carry[h]) for h in range(2))
    outs = [acc * (1.0 / acc[:, MLA_V:MLA_V + 1]) for (_, acc) in carry]
    lane = lax.broadcasted_iota(jnp.int32, outs[0].shape, 1)
    o_ref[...] = jnp.where(lane < MLA_V, outs[0], pltpu.roll(outs[1], MLA_V, 1)).astype(o_ref.dtype)


def _attention(q, k, v, *, latent, batch, seq, ctx, tq, kc):
    hp = MLA_HEADS // 2
    w = 2 * MLA_SLOT
    wv = 2 * MLA_V
    if latent:
        nq = seq // tq
        q_spec = pl.BlockSpec((tq, w), lambda b, h, i: (b * nq + i, h))
        kv_spec = pl.BlockSpec((seq + ctx, w), lambda b, h, i: (b, h))
    else:
        nq = ctx // tq
        q0 = batch * seq // tq
        per_batch = (seq + ctx) // ctx
        q_spec = pl.BlockSpec((tq, w), lambda b, h, i: (q0 + b * nq + i, h))
        kv_spec = pl.BlockSpec((ctx, w), lambda b, h, i: (b * per_batch + seq // ctx, h))
    return pl.pallas_call(
        functools.partial(_attn_kernel, kc=kc),
        grid=(batch, hp, nq),
        in_specs=[q_spec, kv_spec, kv_spec],
        out_specs=pl.BlockSpec((tq, wv), lambda b, h, i: (b * nq + i, h)),
        out_shape=jax.ShapeDtypeStruct((batch * nq * tq, hp * wv), BF16),
        compiler_params=_params("parallel", "parallel", "arbitrary"),
        name="mla_latent" if latent else "mla_context",
    )(q, k, v)


def _merge_kernel(*refs, n_lat_tiles, with_ctx):
    if with_ctx:
        (x_ref, mod_ref, gpre_ref, gpost_ref, wgt_ref, ya_ref, yb_ref, yc_ref, ybc_ref, ycc_ref,
         wb_ref, wo_ref, o_ref) = refs
        is_ctx = pl.program_id(0) >= n_lat_tiles
        yb = jnp.where(is_ctx, ybc_ref[...], yb_ref[...])
        yc = jnp.where(is_ctx, ycc_ref[...], yc_ref[...])
    else:
        x_ref, mod_ref, gpre_ref, gpost_ref, wgt_ref, ya_ref, yb_ref, yc_ref, wb_ref, wo_ref, o_ref = refs
        yb, yc = yb_ref[...], yc_ref[...]
    x = x_ref[...]
    d = x.shape[1]
    h = _modulated(x, gpre_ref[...], mod_ref, 1).astype(BF16)
    ys = (ya_ref[...], yb.astype(BF16), yc)
    m = jnp.zeros(x.shape, F32)
    for br in range(3):
        gate = jax.nn.sigmoid(_dot(h, wgt_ref[:, br * d:(br + 1) * d]))
        m = m + gate * _dot(ys[br], wb_ref[br])
    out = _dot(m.astype(BF16), wo_ref[...])
    o_ref[...] = x + mod_ref[5:6, :] * _rms(out, gpost_ref[...])


def _merge(x_all, mods, layer, g_pre, g_post, w_gates, ya, yb, yc, yb_ctx, yc_ctx, wb, wo,
           *, n_rows, n_lat, tm, cond_of_tile, kv_of_tile):
    d = x_all.shape[1]
    bw = ya.shape[1]
    with_ctx = yb_ctx is not None
    n_lat_tiles = n_lat // tm
    row = lambda w: pl.BlockSpec((tm, w), lambda i: (i, 0))
    lat = lambda w: pl.BlockSpec((tm, w), lambda i: (jnp.minimum(i, n_lat_tiles - 1), 0))
    ctx = lambda w: pl.BlockSpec((tm, w), lambda i: (jnp.maximum(i - n_lat_tiles, 0), 0))
    ya_spec = pl.BlockSpec((tm, bw), lambda i: (kv_of_tile(i), 0))
    branch_specs = [ya_spec, lat(bw), lat(bw)] + ([ctx(bw), ctx(bw)] if with_ctx else [])
    branch_args = [ya, yb, yc] + ([yb_ctx, yc_ctx] if with_ctx else [])
    return pl.pallas_call(
        functools.partial(_merge_kernel, n_lat_tiles=n_lat_tiles, with_ctx=with_ctx),
        grid=(n_rows // tm,),
        in_specs=[
            row(d),
            pl.BlockSpec((None, None, N_MOD, d), lambda i: (layer, cond_of_tile(i), 0, 0)),
            _const_spec((1, d)),
            _const_spec((1, d)),
            _const_spec(w_gates.shape),
            *branch_specs,
            pl.BlockSpec((None,) + wb.shape[1:], lambda i: (layer, 0, 0, 0), pipeline_mode=pl.Buffered(1)),
            pl.BlockSpec((None,) + wo.shape[1:], lambda i: (layer, 0, 0), pipeline_mode=pl.Buffered(1)),
        ],
        out_specs=row(d),
        out_shape=jax.ShapeDtypeStruct((n_rows, d), F32),
        compiler_params=_params("parallel"),
        name="merge",
    )(x_all, mods, g_pre.reshape(1, d), g_post.reshape(1, d), w_gates, *branch_args, wb, wo)


def _rope_table(seq, tm):
    pos = np.arange(seq)
    half = MLA_ROPE // 2
    inv_freq = ROPE_BASE ** (-np.arange(0, half, 2) / half)
    ar, ac = (pos // GRID_W)[:, None] * inv_freq, (pos % GRID_W)[:, None] * inv_freq
    cos = np.concatenate([np.cos(ar), np.cos(ar), np.cos(ac), np.cos(ac)], axis=1)
    sin = np.concatenate([-np.sin(ar), np.sin(ar), -np.sin(ac), np.sin(ac)], axis=1)
    pad = LANES - MLA_NOPE - MLA_ROPE
    rows = lambda n, c, s: np.concatenate(
        [np.ones((n, MLA_NOPE)), c, np.zeros((n, pad)), np.zeros((n, MLA_NOPE)), s, np.zeros((n, pad))], axis=1)
    tab = rows(seq, cos, sin)
    ident = rows(tm, np.ones((tm, MLA_ROPE)), np.zeros((tm, MLA_ROPE)))
    return np.concatenate([tab, ident], axis=0).astype(np.float32)


def _with_rope_partners(w):
    j = np.arange(MLA_ROPE)
    h = MLA_ROPE // 4
    partner = np.where((j % (2 * h)) < h, j + h, j - h)
    rope = w[..., -MLA_ROPE:]
    return jnp.concatenate([w, rope[..., partner]], axis=-1)


def _pad_cols(w, n):
    return jnp.pad(w, ((0, 0), (0, n - w.shape[1])))


def _layer_weights(layer, w_in, gla_w_decay, gla_b_decay, mla_w_uq, mla_w_ukv):
    nqk = GLA_HEADS * GLA_DK
    r = GLA_GATE_RANK
    o_dec = 2 * nqk + 2 * GLA_HEADS * GLA_DV
    o_four = o_dec + 2 * r
    o_cq = o_four + FOURIER_GROUPS * FOURIER_CH
    q_rank = mla_w_uq.shape[1]
    kv_rank = mla_w_ukv.shape[1]
    o_ckv = o_cq + q_rank
    o_kr = o_ckv + kv_rank
    o_gates = o_kr + MLA_ROPE
    w = w_in[layer]
    w_in2 = jnp.concatenate([
        w[:, :o_dec], _pad_cols(w[:, o_dec:o_four], LANES), w[:, o_four:o_cq], w[:, o_cq:o_ckv],
        w[:, o_ckv:o_kr], _pad_cols(_with_rope_partners(w[:, o_kr:o_gates]), LANES)], axis=1).astype(BF16)
    w_gates = w[:, o_gates:].astype(BF16)
    wd = gla_w_decay[layer]
    wdec = jnp.zeros((LANES, 2 * nqk), F32)
    wdec = wdec.at[0:r, 0:nqk].set(wd[0]).at[r:2 * r, nqk:].set(wd[1]).astype(BF16)
    bdec = gla_b_decay[layer].reshape(1, 2 * nqk)
    dq = MLA_NOPE + MLA_ROPE
    wuq = _with_rope_partners(mla_w_uq[layer].reshape(q_rank, MLA_HEADS, dq))
    wuq = wuq.reshape(q_rank, MLA_HEADS * MLA_SLOT).astype(BF16)
    wkv = mla_w_ukv[layer].reshape(kv_rank, MLA_HEADS, MLA_NOPE + MLA_V)
    wk = jnp.pad(wkv[:, :, :MLA_NOPE], ((0, 0), (0, 0), (0, MLA_SLOT - MLA_NOPE)))
    wv = jnp.pad(wkv[:, :, MLA_NOPE:], ((0, 0), (0, 0), (0, MLA_SLOT - MLA_V)))
    wukv = jnp.concatenate([wk.reshape(kv_rank, MLA_HEADS * MLA_SLOT),
                            wv.reshape(kv_rank, MLA_HEADS * MLA_SLOT)], axis=1).astype(BF16)
    return w_in2, w_gates, wdec, bdec, wuq, wukv


def kernel(x, c, ctx, c_ctx, w_mod, b_mod, norm_pre, norm_post, ffn_w_gate, ffn_w_up, ffn_w_down, w_in,
           gla_w_decay, gla_b_decay, gla_norm, mla_q_norm, mla_w_uq, mla_kv_norm, mla_w_ukv, w_branch, w_out):
    batch, seq, d = x.shape
    n_ctx = ctx.shape[1]
    depth = w_mod.shape[0]
    n_lat = batch * seq
    t = n_lat + batch * n_ctx
    tm = min(256, n_ctx)
    tm_ffn = FFN_TM if seq % FFN_TM == 0 and (batch * n_ctx) % FFN_TM == 0 else tm
    d_ff = ffn_w_gate.shape[-1]
    fc = d_ff // 2 if (d_ff // 2) % LANES == 0 else d_ff
    assert seq % tm == 0 and n_ctx % tm == 0 and seq % (FFT_NA * 8) == 0 and batch < MOD_ROWS

    cond_of_tile = lambda i: jnp.minimum((i * tm) // seq, batch)
    tab_of_tile = lambda i: jnp.where(i * tm < n_lat, ((i * tm) % seq) // tm, seq // tm)
    lat_tiles, ctx_tiles = seq // tm, n_ctx // tm

    def kv_of_tile(i):
        j = i - batch * lat_tiles
        lat = (i // lat_tiles) * (lat_tiles + ctx_tiles) + i % lat_tiles
        ctx_row = (j // ctx_tiles) * (lat_tiles + ctx_tiles) + lat_tiles + j % ctx_tiles
        return jnp.where(j < 0, lat, ctx_row)

    n_keys = seq + n_ctx
    key_chunk = max(kc for kc in range(LANES, min(ATTN_KC, n_keys) + 1, LANES) if n_keys % kc == 0)

    cond = jnp.concatenate([c, c_ctx[None, :], jnp.zeros((MOD_ROWS - batch - 1, d), F32)], axis=0)
    mods = _modulation(cond, w_mod, b_mod, tn=(N_MOD * d) // 8).reshape(depth, MOD_ROWS, N_MOD, d)
    tab = _rope_table(seq, tm)
    x_all, x_ctx = x.reshape(n_lat, d), ctx.reshape(batch * n_ctx, d)
    wg_all, wu_all, wd_all = ffn_w_gate.astype(BF16), ffn_w_up.astype(BF16), ffn_w_down.astype(BF16)
    wb_all, wo_all = w_branch.astype(BF16), w_out.astype(BF16)

    for layer in range(depth):
        last = layer == depth - 1
        ffn = lambda xa, xc, sub, s, rows: _half_ffn(
            xa, xc, mods, layer, sub, s, norm_pre[layer, sub], norm_post[layer, sub], wg_all, wu_all, wd_all,
            n_rows=rows, tm=tm_ffn, cond_of_tile=lambda i: jnp.minimum((i * tm_ffn) // seq, batch), fc=fc)
        x_all = ffn(x_all, x_ctx if layer == 0 else None, 0, 0, t)

        w_in2, w_gates, wdec, bdec, wuq, wukv = _layer_weights(
            layer, w_in, gla_w_decay, gla_b_decay, mla_w_uq, mla_w_ukv)
        gla_in, fz, q, k, v = _in_projection(
            x_all, mods, layer, norm_pre[layer, 1], w_in2, wdec, bdec,
            mla_q_norm[layer].reshape(1, -1), wuq, mla_kv_norm[layer].reshape(1, -1), wukv, tab,
            tm=tm, cond_of_tile=cond_of_tile, tab_of_tile=tab_of_tile, kv_of_tile=kv_of_tile)

        scan = functools.partial(_gla_scan, tb=tm, seq=seq, ctx=n_ctx)
        gla_in = gla_in.reshape(batch, n_keys, gla_in.shape[1])
        o_fwd = scan(gla_in, None, None, reverse=False)
        ya = scan(gla_in, o_fwd, gla_norm[layer], reverse=True).reshape(batch * n_keys, -1)

        rows_out = n_lat if last else t
        yb = _fourier_latent(fz, batch=batch, seq=seq)
        yc = _attention(q, k, v, latent=True, batch=batch, seq=seq, ctx=n_ctx,
                        tq=min(ATTN_TQ, seq), kc=key_chunk)
        yb_ctx = yc_ctx = None
        if not last:
            yb_ctx = _fourier_context(fz, batch=batch, seq=seq, ctx=n_ctx)
            yc_ctx = _attention(q, k, v, latent=False, batch=batch, seq=seq, ctx=n_ctx, tq=tm, kc=n_ctx)

        x_all = _merge(x_all, mods, layer, norm_pre[layer, 1], norm_post[layer, 1], w_gates, ya, yb, yc,
                       yb_ctx, yc_ctx, wb_all, wo_all,
                       n_rows=rows_out, n_lat=n_lat, tm=tm, cond_of_tile=cond_of_tile, kv_of_tile=kv_of_tile)
        x_all = ffn(x_all, None, 2, 1, rows_out)
    return x_all.reshape(batch, seq, d)
```

```python
import functools

import jax
import jax.numpy as jnp
import numpy as np
from jax import lax
from jax.experimental import pallas as pl
from jax.experimental.pallas import tpu as pltpu

F32 = jnp.float32
BF16 = jnp.bfloat16

GRID_W = 64
N_MOD = 9
GLA_HEADS = 4
GLA_DK = 64
GLA_DV = 128
GLA_GATE_RANK = 16
GLA_TAU = 16.0
GLA_CHUNK = 64
GLA_EXP_CLAMP = 80.0
FOURIER_GROUPS = 4
FOURIER_CH = 128
FFT_NA = 64
MLA_HEADS = 8
MLA_NOPE = 64
MLA_ROPE = 32
MLA_V = 64
MLA_SLOT = 128
ROPE_BASE = 10000.0
EPS = 1e-6
LOG2_E = 1.4426950408889634
FFN_TM = 512
ATTN_TQ = 512
ATTN_KC = 4096
LANES = 128
MOD_ROWS = 8
VMEM_LIMIT = 56 * 1024 * 1024

NT = (((1,), (1,)), ((), ()))
TN = (((0,), (0,)), ((), ()))


def _params(*sem):
    return pltpu.CompilerParams(dimension_semantics=sem, vmem_limit_bytes=VMEM_LIMIT)


def _const_spec(shape):
    nd = len(shape)
    return pl.BlockSpec(shape, lambda *_: (0,) * nd, pipeline_mode=pl.Buffered(1))


def _dot(a, b):
    return jnp.dot(a, b, preferred_element_type=F32)


def _rms(x, g):
    return x * lax.rsqrt(jnp.mean(x * x, axis=-1, keepdims=True) + EPS) * g


def _silu(x):
    return x * jax.nn.sigmoid(x)


def _modulated(x, g, mod_ref, sub):
    shift = mod_ref[3 * sub + 0:3 * sub + 1, :]
    scale = mod_ref[3 * sub + 1:3 * sub + 2, :]
    return _rms(x, g) * (1.0 + scale) + shift


def _mod_kernel(c_ref, w_ref, b_ref, o_ref):
    a = _silu(c_ref[...]).astype(BF16)
    o_ref[0] = _dot(a, w_ref[0].astype(BF16)) + b_ref[0]


def _modulation(cond, w_mod, b_mod, tn):
    depth, d, nm = w_mod.shape
    return pl.pallas_call(
        _mod_kernel,
        grid=(depth, nm // tn),
        in_specs=[
            pl.BlockSpec((MOD_ROWS, d), lambda l, j: (0, 0)),
            pl.BlockSpec((1, d, tn), lambda l, j: (l, 0, j)),
            pl.BlockSpec((1, 1, tn), lambda l, j: (l, 0, j)),
        ],
        out_specs=pl.BlockSpec((1, MOD_ROWS, tn), lambda l, j: (l, 0, j)),
        out_shape=jax.ShapeDtypeStruct((depth, MOD_ROWS, nm), F32),
        compiler_params=_params("parallel", "parallel"),
        name="modulation",
    )(cond, w_mod, b_mod.reshape(depth, 1, nm))


def _ffn_kernel(*refs, sub, fc, n_lat_tiles):
    if n_lat_tiles is None:
        x_ref, mod_ref, gpre_ref, gpost_ref, wg_ref, wu_ref, wd_ref, o_ref = refs
        xc_ref = None
    else:
        x_ref, xc_ref, mod_ref, gpre_ref, gpost_ref, wg_ref, wu_ref, wd_ref, o_ref = refs
    d_ff = wg_ref.shape[1]
    gate = mod_ref[3 * sub + 2:3 * sub + 3, :]
    tm = x_ref.shape[0]
    half = tm // 2 if tm >= 512 else tm
    xs, hs = [], []
    for r0 in range(0, tm, half):
        rows = slice(r0, r0 + half)
        x = x_ref[rows, :]
        if xc_ref is not None:
            x = jnp.where(pl.program_id(0) >= n_lat_tiles, xc_ref[rows, :], x)
        xs.append(x)
        hs.append(_modulated(x, gpre_ref[...], mod_ref, sub).astype(BF16))
    ys = [jnp.zeros(x.shape, F32) for x in xs]
    for c0 in range(0, d_ff, fc):
        gs = [_dot(h, wg_ref[:, c0:c0 + fc]) for h in hs]
        us = [_dot(h, wu_ref[:, c0:c0 + fc]) for h in hs]
        acts = [(_silu(g) * u).astype(BF16) for g, u in zip(gs, us)]
        ys = [y + _dot(a, wd_ref[c0:c0 + fc, :]) for y, a in zip(ys, acts)]
    for i, (x, y) in enumerate(zip(xs, ys)):
        o_ref[i * half:(i + 1) * half, :] = x + 0.5 * gate * _rms(y, gpost_ref[...])


def _half_ffn(x_all, x_ctx, mods, layer, sub, ffn_idx, g_pre, g_post, wg, wu, wd,
              *, n_rows, tm, cond_of_tile, fc):
    d = x_all.shape[1]
    d_ff = wg.shape[-1]
    stacked = lambda r, c: pl.BlockSpec((None, None, r, c), lambda i: (layer, ffn_idx, 0, 0),
                                        pipeline_mode=pl.Buffered(1))
    if x_ctx is None:
        n_lat_tiles = None
        x_specs, xs = [pl.BlockSpec((tm, d), lambda i: (i, 0))], [x_all]
    else:
        n_lat_tiles = x_all.shape[0] // tm
        x_specs = [pl.BlockSpec((tm, d), lambda i: (jnp.minimum(i, n_lat_tiles - 1), 0)),
                   pl.BlockSpec((tm, d), lambda i: (jnp.maximum(i - n_lat_tiles, 0), 0))]
        xs = [x_all, x_ctx]
    return pl.pallas_call(
        functools.partial(_ffn_kernel, sub=sub, fc=fc, n_lat_tiles=n_lat_tiles),
        grid=(n_rows // tm,),
        in_specs=[
            *x_specs,
            pl.BlockSpec((None, None, N_MOD, d), lambda i: (layer, cond_of_tile(i), 0, 0)),
            _const_spec((1, d)),
            _const_spec((1, d)),
            stacked(d, d_ff),
            stacked(d, d_ff),
            stacked(d_ff, d),
        ],
        out_specs=pl.BlockSpec((tm, d), lambda i: (i, 0)),
        out_shape=jax.ShapeDtypeStruct((n_rows, d), F32),
        compiler_params=_params("parallel"),
        name=f"half_ffn_{sub}",
    )(*xs, mods, g_pre.reshape(1, d), g_post.reshape(1, d), wg, wu, wd)


_Z_GLA = 0
_Z_DEC = 1536
_Z_FOUR = 1664
_Z_CQ = 2176
_Z_CKV = 2560
_Z_KR = 2816
_Z_END = 2944


def _rope_slot(x, tab):
    return x * tab[:, :LANES] + pltpu.roll(x, LANES - MLA_ROPE, 1) * tab[:, LANES:]


def _inproj_kernel(x_ref, mod_ref, gpre_ref, win_ref, wdec_ref, bdec_ref, qn_ref, wuq_ref,
                   kvn_ref, wukv_ref, tab_ref, gla_ref, fz_ref, q_ref, k_ref, v_ref):
    h = _modulated(x_ref[...], gpre_ref[...], mod_ref, 1).astype(BF16)
    zm = _dot(h, win_ref[:, _Z_CQ:_Z_END])
    z = _dot(h, win_ref[:, 0:_Z_CQ])
    cq = _rms(zm[:, 0:_Z_CKV - _Z_CQ], qn_ref[...]).astype(BF16)
    ckv = _rms(zm[:, _Z_CKV - _Z_CQ:_Z_KR - _Z_CQ], kvn_ref[...]).astype(BF16)
    q = _dot(cq, wuq_ref[...])
    kv = _dot(ckv, wukv_ref[...])
    nqk = GLA_HEADS * GLA_DK
    gla_ref[:, 0:nqk] = z[:, 0:nqk] * (GLA_DK ** -0.5)
    gla_ref[:, nqk:_Z_DEC] = z[:, nqk:_Z_DEC]
    xd = _dot(z[:, _Z_DEC:_Z_FOUR].astype(BF16), wdec_ref[...]) + bdec_ref[...]
    logsig = jnp.minimum(xd, 0.0) - jnp.log1p(jnp.exp(-jnp.abs(xd)))
    gla_ref[:, _Z_DEC:_Z_DEC + 2 * nqk] = logsig * (1.0 / GLA_TAU)
    fz_ref[...] = z[:, _Z_FOUR:_Z_CQ]
    tab = tab_ref[...]
    nk = MLA_HEADS * MLA_SLOT
    kr = _rope_slot(pltpu.roll(zm[:, _Z_KR - _Z_CQ:], MLA_NOPE, 1), tab)
    scale = (MLA_NOPE + MLA_ROPE) ** -0.5 * LOG2_E
    for hd in range(MLA_HEADS):
        sl = slice(hd * MLA_SLOT, (hd + 1) * MLA_SLOT)
        q_ref[:, sl] = (_rope_slot(q[:, sl], tab) * scale).astype(BF16)
        k_ref[:, sl] = (kv[:, sl] + kr).astype(BF16)
    lane = lax.broadcasted_iota(jnp.int32, (x_ref.shape[0], nk), 1)
    v_ref[...] = jnp.where(lane % MLA_SLOT == MLA_V, 1.0, kv[:, nk:]).astype(BF16)


def _in_projection(x_all, mods, layer, g_pre, w_in2, wdec, bdec, qn, wuq, kvn, wukv, tab,
                   *, tm, cond_of_tile, tab_of_tile, kv_of_tile):
    t, d = x_all.shape
    nq = MLA_HEADS * MLA_SLOT
    row = lambda w: pl.BlockSpec((tm, w), lambda i: (i, 0))
    kv_row = pl.BlockSpec((tm, nq), lambda i: (kv_of_tile(i), 0))
    gla_row = pl.BlockSpec((tm, 2048), lambda i: (kv_of_tile(i), 0))
    return pl.pallas_call(
        _inproj_kernel,
        grid=(t // tm,),
        in_specs=[
            row(d),
            pl.BlockSpec((None, None, N_MOD, d), lambda i: (layer, cond_of_tile(i), 0, 0)),
            _const_spec((1, d)),
            _const_spec(w_in2.shape),
            _const_spec(wdec.shape),
            _const_spec(bdec.shape),
            _const_spec(qn.shape),
            _const_spec(wuq.shape),
            _const_spec(kvn.shape),
            _const_spec(wukv.shape),
            pl.BlockSpec((tm, 2 * LANES), lambda i: (tab_of_tile(i), 0)),
        ],
        out_specs=[gla_row, row(512), row(nq), kv_row, kv_row],
        out_shape=[
            jax.ShapeDtypeStruct((t, 2048), F32),
            jax.ShapeDtypeStruct((t, 512), F32),
            jax.ShapeDtypeStruct((t, nq), BF16),
            jax.ShapeDtypeStruct((t, nq), BF16),
            jax.ShapeDtypeStruct((t, nq), BF16),
        ],
        compiler_params=_params("parallel"),
        name="in_projection",
    )(x_all, mods, g_pre.reshape(1, d), w_in2, wdec, bdec, qn, wuq, kvn, wukv, tab)


def _gla_kernel(*refs, reverse, final):
    if final:
        g_ref, of_ref, gn_ref, o_ref, st_ref = refs
    else:
        g_ref, o_ref, st_ref = refs
    c = GLA_CHUNK
    n_batch, tb = g_ref.shape[0], g_ref.shape[1]
    nqk = GLA_HEADS * GLA_DK
    nv = GLA_HEADS * GLA_DV

    @pl.when(pl.program_id(0) == 0)
    def _():
        st_ref[...] = jnp.zeros(st_ref.shape, F32)

    row = lax.broadcasted_iota(jnp.int32, (c, c), 0)
    col = lax.broadcasted_iota(jnp.int32, (c, c), 1)
    keep = (col >= row) if reverse else (col <= row)
    lmat = jnp.where(keep, 1.0, 0.0).astype(BF16)
    la_off = _Z_DEC + (nqk if reverse else 0)
    mid = c // 2 if reverse else c // 2 - 1
    end = 0 if reverse else c - 1
    chunks = range(tb // c)
    order = list(reversed(chunks) if reverse else chunks)
    work = [(bi, slice(ci * c, (ci + 1) * c)) for bi in range(n_batch) for ci in order]
    heads = [(slice(hd * GLA_DK, (hd + 1) * GLA_DK), slice(hd * GLA_DV, (hd + 1) * GLA_DV))
             for hd in range(GLA_HEADS)]

    cums = []
    for bi, rows in work:
        la = g_ref[bi, rows, la_off:la_off + nqk]
        h1 = la.astype(BF16)
        r1 = la - h1.astype(F32)
        h2 = r1.astype(BF16)
        h3 = (r1 - h2.astype(F32)).astype(BF16)
        cums.append(_dot(lmat, h1) + _dot(lmat, h2) + _dot(lmat, h3))
    scaled = []
    for (bi, rows), b in zip(work, cums):
        ref = b[mid:mid + 1, :]
        tot = b[end:end + 1, :]
        qt = g_ref[bi, rows, 0:nqk] * jnp.exp(b - ref)
        kt = g_ref[bi, rows, nqk:2 * nqk] * jnp.exp(jnp.minimum(ref - b, GLA_EXP_CLAMP))
        q_in = (qt * jnp.exp(ref)).astype(BF16)
        k_st = (kt * jnp.exp(tot - ref)).astype(BF16)
        scaled.append((qt.astype(BF16), kt.astype(BF16), q_in, k_st, jnp.exp(tot)))
    scores = []
    for qt, kt, _, _, _ in scaled:
        scores.append([jnp.where(keep, lax.dot_general(qt[:, ks], kt[:, ks], NT, preferred_element_type=F32), 0.0)
                       .astype(BF16) for ks, _ in heads])
    intra, upd = [], []
    for (bi, rows), a, (_, _, _, k_st, _) in zip(work, scores, scaled):
        vals = [g_ref[bi, rows, 2 * nqk + vs.start:2 * nqk + vs.stop].astype(BF16) for _, vs in heads]
        intra.append([_dot(a[hd], vals[hd]) for hd in range(GLA_HEADS)])
        upd.append([lax.dot_general(vals[hd], k_st[:, heads[hd][0]], TN, preferred_element_type=F32)
                    for hd in range(GLA_HEADS)])
    states = {bi: [st_ref[bi, hd] for hd in range(GLA_HEADS)] for bi in range(n_batch)}
    for (bi, rows), o_in, u, (_, _, q_in, _, dec) in zip(work, intra, upd, scaled):
        for hd, (ks, vs) in enumerate(heads):
            st = states[bi][hd]
            o = o_in[hd] + lax.dot_general(q_in[:, ks], st.astype(BF16), NT, preferred_element_type=F32)
            states[bi][hd] = st * dec[:, ks] + u[hd]
            if final:
                o = o + of_ref[bi, rows, vs]
                gate = g_ref[bi, rows, 2 * nqk + nv + vs.start:2 * nqk + nv + vs.stop]
                o_ref[bi, rows, vs] = (_rms(o, gn_ref[...]) * _silu(gate)).astype(o_ref.dtype)
            else:
                o_ref[bi, rows, vs] = o
    for bi in range(n_batch):
        for hd in range(GLA_HEADS):
            st_ref[bi, hd] = states[bi][hd]


def _gla_scan(gla_in, o_fwd, g_norm, *, reverse, tb, seq, ctx):
    batch = gla_in.shape[0]
    n_lat, n_ctx = seq // tb, ctx // tb

    def blk(j):
        if reverse:
            return jnp.where(j < n_ctx, n_lat + (n_ctx - 1 - j), n_lat + n_ctx - 1 - j)
        return jnp.where(j < n_ctx, n_lat + j, j - n_ctx)

    final = o_fwd is not None
    dv = GLA_HEADS * GLA_DV
    in_specs = [pl.BlockSpec((batch, tb, gla_in.shape[2]), lambda j: (0, blk(j), 0))]
    args = [gla_in]
    if final:
        in_specs += [pl.BlockSpec((batch, tb, dv), lambda j: (0, blk(j), 0)), _const_spec((1, GLA_DV))]
        args += [o_fwd, g_norm.reshape(1, GLA_DV)]
    return pl.pallas_call(
        functools.partial(_gla_kernel, reverse=reverse, final=final),
        grid=(n_lat + n_ctx,),
        in_specs=in_specs,
        out_specs=pl.BlockSpec((batch, tb, dv), lambda j: (0, blk(j), 0)),
        out_shape=jax.ShapeDtypeStruct((batch, seq + ctx, dv), BF16 if final else F32),
        scratch_shapes=[pltpu.VMEM((batch, GLA_HEADS, GLA_DV, GLA_DK), F32)],
        compiler_params=_params("arbitrary"),
        name="gla_bwd_scan" if reverse else "gla_fwd_scan",
    )(*args)


def _dft_mats(n):
    k = np.arange(n)
    ang = (2.0 * np.pi / n) * ((k[:, None] * k[None, :]) % n)
    return np.cos(ang), np.sin(ang)


def _fft_kernel(x_ref, fa_ref, twc_ref, tws_ref, mb_ref, cs_ref, o_ref, zr_ref, zi_ref, *, na, nb, bg, kg, pitch):
    ch = x_ref.shape[1]
    fa = fa_ref[...].astype(BF16)
    for g0 in range(0, nb, bg * kg):
        starts = range(g0, min(g0 + bg * kg, nb), bg)
        xs = [jnp.concatenate([x_ref[pl.ds(b0 + i, na, stride=nb), :] for i in range(bg)], axis=1).astype(BF16)
              for b0 in starts]
        zs = [_dot(fa, x) for x in xs]
        for b0, z in zip(starts, zs):
            for i in range(bg):
                r0 = (b0 + i) * pitch
                zr_ref[r0:r0 + na, :] = z[:na, i * ch:(i + 1) * ch]
                zi_ref[r0:r0 + na, :] = z[na:, i * ch:(i + 1) * ch]
    mb = mb_ref[...].astype(BF16)
    cs = cs_ref[...].astype(BF16)

    for k0 in range(0, na, kg):
        group = range(k0, min(k0 + kg, na))
        ts = []
        for ka in group:
            zr, zi = zr_ref[pl.ds(ka, nb, stride=pitch), :], zi_ref[pl.ds(ka, nb, stride=pitch), :]
            c = twc_ref[ka * nb:(ka + 1) * nb, :]
            s = tws_ref[ka * nb:(ka + 1) * nb, :]
            ts.append(jnp.concatenate([zr * c + zi * s, zi * c - zr * s], axis=0).astype(BF16))
        vs = [_dot(mb, t) for t in ts]
        vvs = [jnp.concatenate([v[:nb], v[nb:]], axis=1).astype(BF16) for v in vs]
        outs = [_dot(vv, cs) for vv in vvs]
        for ka, out in zip(group, outs):
            o_ref[pl.ds(ka, nb, stride=na), :] = out


def _fourier_latent(fz, *, batch, seq):
    cw = fz.shape[1]
    na, nb = FFT_NA, seq // FFT_NA
    pitch = na + 1
    ch = FOURIER_CH
    ca, sa = _dft_mats(na)
    f_a = np.concatenate([ca, -sa], axis=0).astype(np.float32)
    cb, sb = _dft_mats(nb)
    m_b = np.concatenate([np.concatenate([cb, sb], axis=1),
                          np.concatenate([-sb, cb], axis=1)], axis=0).astype(np.float32)
    cc, sc = _dft_mats(ch)
    cs = (np.concatenate([cc, sc], axis=0) * ((seq * ch) ** -0.5)).astype(np.float32)
    ang = (2.0 * np.pi / seq) * ((np.arange(na)[:, None] * np.arange(nb)[None, :]) % seq)
    twc = jnp.broadcast_to(np.cos(ang).reshape(seq, 1).astype(np.float32), (seq, ch))
    tws = jnp.broadcast_to(np.sin(ang).reshape(seq, 1).astype(np.float32), (seq, ch))
    return pl.pallas_call(
        functools.partial(_fft_kernel, na=na, nb=nb, bg=4, kg=8, pitch=pitch),
        grid=(batch, cw // ch),
        in_specs=[pl.BlockSpec((seq, ch), lambda b, g: (b, g)), _const_spec(f_a.shape),
                  _const_spec((seq, ch)), _const_spec((seq, ch)), _const_spec(m_b.shape), _const_spec(cs.shape)],
        out_specs=pl.BlockSpec((seq, ch), lambda b, g: (b, g)),
        out_shape=jax.ShapeDtypeStruct((batch * seq, cw), F32),
        scratch_shapes=[pltpu.VMEM((nb * pitch, ch), F32), pltpu.VMEM((nb * pitch, ch), F32)],
        compiler_params=_params("parallel", "parallel"),
        name="fnet_latent",
    )(fz, f_a, twc, tws, m_b, cs)


def _fft_ctx_kernel(x_ref, f_ref, cs_ref, o_ref):
    ch = FOURIER_CH
    f = f_ref[...].astype(BF16)
    cs = cs_ref[...].astype(BF16)
    for g in range(x_ref.shape[1] // ch):
        ab = _dot(x_ref[:, g * ch:(g + 1) * ch].astype(BF16), cs)
        st = jnp.concatenate([ab[:, :ch], ab[:, ch:]], axis=0).astype(BF16)
        o_ref[:, g * ch:(g + 1) * ch] = _dot(f, st)


def _fourier_context(fz, *, batch, seq, ctx):
    cw = fz.shape[1]
    ch = FOURIER_CH
    cn, sn = _dft_mats(ctx)
    f = (np.concatenate([cn, -sn], axis=1) * ((ctx * ch) ** -0.5)).astype(np.float32)
    cc, sc = _dft_mats(ch)
    cs = np.concatenate([cc, sc], axis=1).astype(np.float32)
    blk0 = batch * seq // ctx
    return pl.pallas_call(
        _fft_ctx_kernel,
        grid=(batch,),
        in_specs=[
            pl.BlockSpec((ctx, cw), lambda b: (blk0 + b, 0)),
            _const_spec(f.shape),
            _const_spec(cs.shape),
        ],
        out_specs=pl.BlockSpec((ctx, cw), lambda b: (b, 0)),
        out_shape=jax.ShapeDtypeStruct((batch * ctx, cw), F32),
        compiler_params=_params("parallel"),
        name="fnet_context",
    )(fz, f, cs)


def _attn_update(s, v, carry):
    m, acc = carry
    m_new = jnp.maximum(m, jnp.max(s, axis=-1, keepdims=True))
    p = jnp.exp2(s - m_new).astype(BF16)
    acc = jnp.exp2(m - m_new) * acc + _dot(p, v)
    return m_new, acc


def _attn_kernel(q_ref, k_ref, v_ref, o_ref, *, kc):
    tq = q_ref.shape[0]
    sl = [slice(0, MLA_SLOT), slice(MLA_SLOT, 2 * MLA_SLOT)]
    qs = [q_ref[:, s] for s in sl]
    init = (jnp.full((tq, 1), -jnp.inf, F32), jnp.zeros((tq, MLA_SLOT), F32))
    carry = (init, init)
    starts = list(range(0, k_ref.shape[0], kc))

    def scores(c0):
        return [lax.dot_general(qs[h], k_ref[c0:c0 + kc, sl[h]], NT, preferred_element_type=F32) for h in range(2)]

    s_next = scores(starts[0])
    for i, c0 in enumerate(starts):
        s_cur = s_next
        if i + 1 < len(starts):
            s_next = scores(starts[i + 1])
        carry = tuple(_attn_update(s_cur[h], v_ref[c0:c0 + kc, sl[h]], carry[h]) for h in range(2))
    outs = [acc * (1.0 / acc[:, MLA_V:MLA_V + 1]) for (_, acc) in carry]
    lane = lax.broadcasted_iota(jnp.int32, outs[0].shape, 1)
    o_ref[...] = jnp.where(lane < MLA_V, outs[0], pltpu.roll(outs[1], MLA_V, 1)).astype(o_ref.dtype)


def _attention(q, k, v, *, latent, batch, seq, ctx, tq, kc):
    hp = MLA_HEADS // 2
    w = 2 * MLA_SLOT
    wv = 2 * MLA_V
    if latent:
        nq = seq // tq
        q_spec = pl.BlockSpec((tq, w), lambda b, h, i: (b * nq + i, h))
        kv_spec = pl.BlockSpec((seq + ctx, w), lambda b, h, i: (b, h))
    else:
        nq = ctx // tq
        q0 = batch * seq // tq
        per_batch = (seq + ctx) // ctx
        q_spec = pl.BlockSpec((tq, w), lambda b, h, i: (q0 + b * nq + i, h))
        kv_spec = pl.BlockSpec((ctx, w), lambda b, h, i: (b * per_batch + seq // ctx, h))
    return pl.pallas_call(
        functools.partial(_attn_kernel, kc=kc),
        grid=(batch, hp, nq),
        in_specs=[q_spec, kv_spec, kv_spec],
        out_specs=pl.BlockSpec((tq, wv), lambda b, h, i: (b * nq + i, h)),
        out_shape=jax.ShapeDtypeStruct((batch * nq * tq, hp * wv), BF16),
        compiler_params=_params("parallel", "parallel", "arbitrary"),
        name="mla_latent" if latent else "mla_context",
    )(q, k, v)


def _merge_kernel(*refs, n_lat_tiles, with_ctx):
    if with_ctx:
        (x_ref, mod_ref, gpre_ref, gpost_ref, wgt_ref, ya_ref, yb_ref, yc_ref, ybc_ref, ycc_ref,
         wb_ref, wo_ref, o_ref) = refs
        is_ctx = pl.program_id(0) >= n_lat_tiles
        yb = jnp.where(is_ctx, ybc_ref[...], yb_ref[...])
        yc = jnp.where(is_ctx, ycc_ref[...], yc_ref[...])
    else:
        x_ref, mod_ref, gpre_ref, gpost_ref, wgt_ref, ya_ref, yb_ref, yc_ref, wb_ref, wo_ref, o_ref = refs
        yb, yc = yb_ref[...], yc_ref[...]
    x = x_ref[...]
    d = x.shape[1]
    ys = (ya_ref[...], yb.astype(BF16), yc)
    branches = [_dot(ys[br], wb_ref[br]) for br in range(3)]
    h = _modulated(x, gpre_ref[...], mod_ref, 1).astype(BF16)
    gates = [_dot(h, wgt_ref[:, br * d:(br + 1) * d]) for br in range(3)]
    m = jnp.zeros(x.shape, F32)
    for br in range(3):
        m = m + jax.nn.sigmoid(gates[br]) * branches[br]
    out = _dot(m.astype(BF16), wo_ref[...])
    o_ref[...] = x + mod_ref[5:6, :] * _rms(out, gpost_ref[...])


def _merge(x_all, mods, layer, g_pre, g_post, w_gates, ya, yb, yc, yb_ctx, yc_ctx, wb, wo,
           *, n_rows, n_lat, tm, cond_of_tile, kv_of_tile):
    d = x_all.shape[1]
    bw = ya.shape[1]
    with_ctx = yb_ctx is not None
    n_lat_tiles = n_lat // tm
    row = lambda w: pl.BlockSpec((tm, w), lambda i: (i, 0))
    lat = lambda w: pl.BlockSpec((tm, w), lambda i: (jnp.minimum(i, n_lat_tiles - 1), 0))
    ctx = lambda w: pl.BlockSpec((tm, w), lambda i: (jnp.maximum(i - n_lat_tiles, 0), 0))
    ya_spec = pl.BlockSpec((tm, bw), lambda i: (kv_of_tile(i), 0))
    branch_specs = [ya_spec, lat(bw), lat(bw)] + ([ctx(bw), ctx(bw)] if with_ctx else [])
    branch_args = [ya, yb, yc] + ([yb_ctx, yc_ctx] if with_ctx else [])
    return pl.pallas_call(
        functools.partial(_merge_kernel, n_lat_tiles=n_lat_tiles, with_ctx=with_ctx),
        grid=(n_rows // tm,),
        in_specs=[
            row(d),
            pl.BlockSpec((None, None, N_MOD, d), lambda i: (layer, cond_of_tile(i), 0, 0)),
            _const_spec((1, d)),
            _const_spec((1, d)),
            _const_spec(w_gates.shape),
            *branch_specs,
            pl.BlockSpec((None,) + wb.shape[1:], lambda i: (layer, 0, 0, 0), pipeline_mode=pl.Buffered(1)),
            pl.BlockSpec((None,) + wo.shape[1:], lambda i: (layer, 0, 0), pipeline_mode=pl.Buffered(1)),
        ],
        out_specs=row(d),
        out_shape=jax.ShapeDtypeStruct((n_rows, d), F32),
        compiler_params=_params("parallel"),
        name="merge",
    )(x_all, mods, g_pre.reshape(1, d), g_post.reshape(1, d), w_gates, *branch_args, wb, wo)


def _rope_table(seq, tm):
    pos = np.arange(seq)
    half = MLA_ROPE // 2
    inv_freq = ROPE_BASE ** (-np.arange(0, half, 2) / half)
    ar, ac = (pos // GRID_W)[:, None] * inv_freq, (pos % GRID_W)[:, None] * inv_freq
    cos = np.concatenate([np.cos(ar), np.cos(ar), np.cos(ac), np.cos(ac)], axis=1)
    sin = np.concatenate([-np.sin(ar), np.sin(ar), -np.sin(ac), np.sin(ac)], axis=1)
    pad = LANES - MLA_NOPE - MLA_ROPE
    rows = lambda n, c, s: np.concatenate(
        [np.ones((n, MLA_NOPE)), c, np.zeros((n, pad)), np.zeros((n, MLA_NOPE)), s, np.zeros((n, pad))], axis=1)
    tab = rows(seq, cos, sin)
    ident = rows(tm, np.ones((tm, MLA_ROPE)), np.zeros((tm, MLA_ROPE)))
    return np.concatenate([tab, ident], axis=0).astype(np.float32)


def _with_rope_partners(w):
    j = np.arange(MLA_ROPE)
    h = MLA_ROPE // 4
    partner = np.where((j % (2 * h)) < h, j + h, j - h)
    rope = w[..., -MLA_ROPE:]
    return jnp.concatenate([w, rope[..., partner]], axis=-1)


def _pad_cols(w, n):
    return jnp.pad(w, ((0, 0), (0, n - w.shape[1])))


def _layer_weights(layer, w_in, gla_w_decay, gla_b_decay, mla_w_uq, mla_w_ukv):
    nqk = GLA_HEADS * GLA_DK
    r = GLA_GATE_RANK
    o_dec = 2 * nqk + 2 * GLA_HEADS * GLA_DV
    o_four = o_dec + 2 * r
    o_cq = o_four + FOURIER_GROUPS * FOURIER_CH
    q_rank = mla_w_uq.shape[1]
    kv_rank = mla_w_ukv.shape[1]
    o_ckv = o_cq + q_rank
    o_kr = o_ckv + kv_rank
    o_gates = o_kr + MLA_ROPE
    w = w_in[layer]
    w_in2 = jnp.concatenate([
        w[:, :o_dec], _pad_cols(w[:, o_dec:o_four], LANES), w[:, o_four:o_cq], w[:, o_cq:o_ckv],
        w[:, o_ckv:o_kr], _pad_cols(_with_rope_partners(w[:, o_kr:o_gates]), LANES)], axis=1).astype(BF16)
    w_gates = w[:, o_gates:].astype(BF16)
    wd = gla_w_decay[layer]
    wdec = jnp.zeros((LANES, 2 * nqk), F32)
    wdec = wdec.at[0:r, 0:nqk].set(wd[0]).at[r:2 * r, nqk:].set(wd[1]).astype(BF16)
    bdec = gla_b_decay[layer].reshape(1, 2 * nqk)
    dq = MLA_NOPE + MLA_ROPE
    wuq = _with_rope_partners(mla_w_uq[layer].reshape(q_rank, MLA_HEADS, dq))
    wuq = wuq.reshape(q_rank, MLA_HEADS * MLA_SLOT).astype(BF16)
    wkv = mla_w_ukv[layer].reshape(kv_rank, MLA_HEADS, MLA_NOPE + MLA_V)
    wk = jnp.pad(wkv[:, :, :MLA_NOPE], ((0, 0), (0, 0), (0, MLA_SLOT - MLA_NOPE)))
    wv = jnp.pad(wkv[:, :, MLA_NOPE:], ((0, 0), (0, 0), (0, MLA_SLOT - MLA_V)))
    wukv = jnp.concatenate([wk.reshape(kv_rank, MLA_HEADS * MLA_SLOT),
                            wv.reshape(kv_rank, MLA_HEADS * MLA_SLOT)], axis=1).astype(BF16)
    return w_in2, w_gates, wdec, bdec, wuq, wukv


def kernel(x, c, ctx, c_ctx, w_mod, b_mod, norm_pre, norm_post, ffn_w_gate, ffn_w_up, ffn_w_down, w_in,
           gla_w_decay, gla_b_decay, gla_norm, mla_q_norm, mla_w_uq, mla_kv_norm, mla_w_ukv, w_branch, w_out):
    batch, seq, d = x.shape
    n_ctx = ctx.shape[1]
    depth = w_mod.shape[0]
    n_lat = batch * seq
    t = n_lat + batch * n_ctx
    tm = min(256, n_ctx)
    tm_ffn = FFN_TM if seq % FFN_TM == 0 and (batch * n_ctx) % FFN_TM == 0 else tm
    d_ff = ffn_w_gate.shape[-1]
    fc = d_ff // 2 if (d_ff // 2) % LANES == 0 else d_ff
    assert seq % tm == 0 and n_ctx % tm == 0 and seq % (FFT_NA * 8) == 0 and batch < MOD_ROWS

    cond_of_tile = lambda i: jnp.minimum((i * tm) // seq, batch)
    tab_of_tile = lambda i: jnp.where(i * tm < n_lat, ((i * tm) % seq) // tm, seq // tm)
    lat_tiles, ctx_tiles = seq // tm, n_ctx // tm

    def kv_of_tile(i):
        j = i - batch * lat_tiles
        lat = (i // lat_tiles) * (lat_tiles + ctx_tiles) + i % lat_tiles
        ctx_row = (j // ctx_tiles) * (lat_tiles + ctx_tiles) + lat_tiles + j % ctx_tiles
        return jnp.where(j < 0, lat, ctx_row)

    n_keys = seq + n_ctx
    key_chunk = max(kc for kc in range(LANES, min(ATTN_KC, n_keys) + 1, LANES) if n_keys % kc == 0)

    cond = jnp.concatenate([c, c_ctx[None, :], jnp.zeros((MOD_ROWS - batch - 1, d), F32)], axis=0)
    mods = _modulation(cond, w_mod, b_mod, tn=(N_MOD * d) // 8).reshape(depth, MOD_ROWS, N_MOD, d)
    tab = _rope_table(seq, tm)
    x_all, x_ctx = x.reshape(n_lat, d), ctx.reshape(batch * n_ctx, d)
    wg_all, wu_all, wd_all = ffn_w_gate.astype(BF16), ffn_w_up.astype(BF16), ffn_w_down.astype(BF16)
    wb_all, wo_all = w_branch.astype(BF16), w_out.astype(BF16)

    for layer in range(depth):
        last = layer == depth - 1
        ffn = lambda xa, xc, sub, s, rows: _half_ffn(
            xa, xc, mods, layer, sub, s, norm_pre[layer, sub], norm_post[layer, sub], wg_all, wu_all, wd_all,
            n_rows=rows, tm=tm_ffn, cond_of_tile=lambda i: jnp.minimum((i * tm_ffn) // seq, batch), fc=fc)
        x_all = ffn(x_all, x_ctx if layer == 0 else None, 0, 0, t)

        w_in2, w_gates, wdec, bdec, wuq, wukv = _layer_weights(
            layer, w_in, gla_w_decay, gla_b_decay, mla_w_uq, mla_w_ukv)
        gla_in, fz, q, k, v = _in_projection(
            x_all, mods, layer, norm_pre[layer, 1], w_in2, wdec, bdec,
            mla_q_norm[layer].reshape(1, -1), wuq, mla_kv_norm[layer].reshape(1, -1), wukv, tab,
            tm=tm, cond_of_tile=cond_of_tile, tab_of_tile=tab_of_tile, kv_of_tile=kv_of_tile)

        scan = functools.partial(_gla_scan, tb=tm, seq=seq, ctx=n_ctx)
        gla_in = gla_in.reshape(batch, n_keys, gla_in.shape[1])
        o_fwd = scan(gla_in, None, None, reverse=False)
        ya = scan(gla_in, o_fwd, gla_norm[layer], reverse=True).reshape(batch * n_keys, -1)

        rows_out = n_lat if last else t
        yb = _fourier_latent(fz, batch=batch, seq=seq)
        yc = _attention(q, k, v, latent=True, batch=batch, seq=seq, ctx=n_ctx,
                        tq=min(ATTN_TQ, seq), kc=key_chunk)
        yb_ctx = yc_ctx = None
        if not last:
            yb_ctx = _fourier_context(fz, batch=batch, seq=seq, ctx=n_ctx)
            yc_ctx = _attention(q, k, v, latent=False, batch=batch, seq=seq, ctx=n_ctx, tq=tm, kc=n_ctx)

        x_all = _merge(x_all, mods, layer, norm_pre[layer, 1], norm_post[layer, 1], w_gates, ya, yb, yc,
                       yb_ctx, yc_ctx, wb_all, wo_all,
                       n_rows=rows_out, n_lat=n_lat, tm=tm, cond_of_tile=cond_of_tile, kv_of_tile=kv_of_tile)
        x_all = ffn(x_all, None, 2, 1, rows_out)
    return x_all.reshape(batch, seq, d)
```

```python
import functools

import jax
import jax.numpy as jnp
import numpy as np
from jax import lax
from jax.experimental import pallas as pl
from jax.experimental.pallas import tpu as pltpu

F32 = jnp.float32
BF16 = jnp.bfloat16

GRID_W = 64
N_MOD = 9
GLA_HEADS = 4
GLA_DK = 64
GLA_DV = 128
GLA_GATE_RANK = 16
GLA_TAU = 16.0
GLA_CHUNK = 64
GLA_EXP_CLAMP = 80.0
FOURIER_GROUPS = 4
FOURIER_CH = 128
FFT_NA = 64
MLA_HEADS = 8
MLA_NOPE = 64
MLA_ROPE = 32
MLA_V = 64
MLA_SLOT = 128
ROPE_BASE = 10000.0
EPS = 1e-6
LOG2_E = 1.4426950408889634
FFN_TM = 512
ATTN_TQ = 512
ATTN_KC = 4096
LANES = 128
MOD_ROWS = 8
VMEM_LIMIT = 56 * 1024 * 1024

NT = (((1,), (1,)), ((), ()))
TN = (((0,), (0,)), ((), ()))


def _params(*sem):
    return pltpu.CompilerParams(dimension_semantics=sem, vmem_limit_bytes=VMEM_LIMIT)


def _const_spec(shape):
    nd = len(shape)
    return pl.BlockSpec(shape, lambda *_: (0,) * nd, pipeline_mode=pl.Buffered(1))


def _dot(a, b):
    return jnp.dot(a, b, preferred_element_type=F32)


def _rms(x, g):
    return x * lax.rsqrt(jnp.mean(x * x, axis=-1, keepdims=True) + EPS) * g


def _silu(x):
    return x * jax.nn.sigmoid(x)


def _modulated(x, g, mod_ref, sub):
    shift = mod_ref[3 * sub + 0:3 * sub + 1, :]
    scale = mod_ref[3 * sub + 1:3 * sub + 2, :]
    return _rms(x, g) * (1.0 + scale) + shift


def _mod_kernel(c_ref, w_ref, b_ref, o_ref):
    a = _silu(c_ref[...]).astype(BF16)
    o_ref[0] = _dot(a, w_ref[0].astype(BF16)) + b_ref[0]


def _modulation(cond, w_mod, b_mod, tn):
    depth, d, nm = w_mod.shape
    return pl.pallas_call(
        _mod_kernel,
        grid=(depth, nm // tn),
        in_specs=[
            pl.BlockSpec((MOD_ROWS, d), lambda l, j: (0, 0)),
            pl.BlockSpec((1, d, tn), lambda l, j: (l, 0, j)),
            pl.BlockSpec((1, 1, tn), lambda l, j: (l, 0, j)),
        ],
        out_specs=pl.BlockSpec((1, MOD_ROWS, tn), lambda l, j: (l, 0, j)),
        out_shape=jax.ShapeDtypeStruct((depth, MOD_ROWS, nm), F32),
        compiler_params=_params("parallel", "parallel"),
        name="modulation",
    )(cond, w_mod, b_mod.reshape(depth, 1, nm))


def _ffn_kernel(*refs, sub, fc, n_lat_tiles):
    if n_lat_tiles is None:
        x_ref, mod_ref, gpre_ref, gpost_ref, wg_ref, wu_ref, wd_ref, o_ref = refs
        xc_ref = None
    else:
        x_ref, xc_ref, mod_ref, gpre_ref, gpost_ref, wg_ref, wu_ref, wd_ref, o_ref = refs
    d_ff = wg_ref.shape[1]
    gate = mod_ref[3 * sub + 2:3 * sub + 3, :]
    tm = x_ref.shape[0]
    half = tm // 2 if tm >= 512 else tm
    xs, hs = [], []
    for r0 in range(0, tm, half):
        rows = slice(r0, r0 + half)
        x = x_ref[rows, :]
        if xc_ref is not None:
            x = jnp.where(pl.program_id(0) >= n_lat_tiles, xc_ref[rows, :], x)
        xs.append(x)
        hs.append(_modulated(x, gpre_ref[...], mod_ref, sub).astype(BF16))
    ys = [jnp.zeros(x.shape, F32) for x in xs]
    for c0 in range(0, d_ff, fc):
        gs = [_dot(h, wg_ref[:, c0:c0 + fc]) for h in hs]
        us = [_dot(h, wu_ref[:, c0:c0 + fc]) for h in hs]
        acts = [(_silu(g) * u).astype(BF16) for g, u in zip(gs, us)]
        ys = [y + _dot(a, wd_ref[c0:c0 + fc, :]) for y, a in zip(ys, acts)]
    for i, (x, y) in enumerate(zip(xs, ys)):
        o_ref[i * half:(i + 1) * half, :] = x + 0.5 * gate * _rms(y, gpost_ref[...])


def _half_ffn(x_all, x_ctx, mods, layer, sub, ffn_idx, g_pre, g_post, wg, wu, wd,
              *, n_rows, tm, cond_of_tile, fc):
    d = x_all.shape[1]
    d_ff = wg.shape[-1]
    stacked = lambda r, c: pl.BlockSpec((None, None, r, c), lambda i: (layer, ffn_idx, 0, 0),
                                        pipeline_mode=pl.Buffered(1))
    if x_ctx is None:
        n_lat_tiles = None
        x_specs, xs = [pl.BlockSpec((tm, d), lambda i: (i, 0))], [x_all]
    else:
        n_lat_tiles = x_all.shape[0] // tm
        x_specs = [pl.BlockSpec((tm, d), lambda i: (jnp.minimum(i, n_lat_tiles - 1), 0)),
                   pl.BlockSpec((tm, d), lambda i: (jnp.maximum(i - n_lat_tiles, 0), 0))]
        xs = [x_all, x_ctx]
    return pl.pallas_call(
        functools.partial(_ffn_kernel, sub=sub, fc=fc, n_lat_tiles=n_lat_tiles),
        grid=(n_rows // tm,),
        in_specs=[
            *x_specs,
            pl.BlockSpec((None, None, N_MOD, d), lambda i: (layer, cond_of_tile(i), 0, 0)),
            _const_spec((1, d)),
            _const_spec((1, d)),
            stacked(d, d_ff),
            stacked(d, d_ff),
            stacked(d_ff, d),
        ],
        out_specs=pl.BlockSpec((tm, d), lambda i: (i, 0)),
        out_shape=jax.ShapeDtypeStruct((n_rows, d), F32),
        compiler_params=_params("parallel"),
        name=f"half_ffn_{sub}",
    )(*xs, mods, g_pre.reshape(1, d), g_post.reshape(1, d), wg, wu, wd)


_Z_GLA = 0
_Z_DEC = 1536
_Z_FOUR = 1664
_Z_CQ = 2176
_Z_CKV = 2560
_Z_KR = 2816
_Z_END = 2944


def _rope_slot(x, tab):
    return x * tab[:, :LANES] + pltpu.roll(x, LANES - MLA_ROPE, 1) * tab[:, LANES:]


def _inproj_kernel(x_ref, mod_ref, gpre_ref, win_ref, wdec_ref, bdec_ref, qn_ref, wuq_ref,
                   kvn_ref, wukv_ref, tab_ref, gla_ref, fz_ref, q_ref, k_ref, v_ref):
    h = _modulated(x_ref[...], gpre_ref[...], mod_ref, 1).astype(BF16)
    zm = _dot(h, win_ref[:, _Z_CQ:_Z_END])
    zd = _dot(h, win_ref[:, _Z_DEC:_Z_FOUR])
    cq = _rms(zm[:, 0:_Z_CKV - _Z_CQ], qn_ref[...]).astype(BF16)
    ckv = _rms(zm[:, _Z_CKV - _Z_CQ:_Z_KR - _Z_CQ], kvn_ref[...]).astype(BF16)
    q = _dot(cq, wuq_ref[...])
    kv = _dot(ckv, wukv_ref[...])
    nqk = GLA_HEADS * GLA_DK
    xd = _dot(zd.astype(BF16), wdec_ref[...]) + bdec_ref[...]
    zg = _dot(h, win_ref[:, 0:_Z_DEC])
    zf = _dot(h, win_ref[:, _Z_FOUR:_Z_CQ])
    logsig = jnp.minimum(xd, 0.0) - jnp.log1p(jnp.exp(-jnp.abs(xd)))
    gla_ref[:, _Z_DEC:_Z_DEC + 2 * nqk] = logsig * (1.0 / GLA_TAU)
    tab = tab_ref[...]
    nk = MLA_HEADS * MLA_SLOT
    kr = _rope_slot(pltpu.roll(zm[:, _Z_KR - _Z_CQ:], MLA_NOPE, 1), tab)
    scale = (MLA_NOPE + MLA_ROPE) ** -0.5 * LOG2_E
    for hd in range(MLA_HEADS):
        sl = slice(hd * MLA_SLOT, (hd + 1) * MLA_SLOT)
        q_ref[:, sl] = (_rope_slot(q[:, sl], tab) * scale).astype(BF16)
        k_ref[:, sl] = (kv[:, sl] + kr).astype(BF16)
    lane = lax.broadcasted_iota(jnp.int32, (x_ref.shape[0], nk), 1)
    v_ref[...] = jnp.where(lane % MLA_SLOT == MLA_V, 1.0, kv[:, nk:]).astype(BF16)
    gla_ref[:, 0:nqk] = zg[:, 0:nqk] * (GLA_DK ** -0.5)
    gla_ref[:, nqk:_Z_DEC] = zg[:, nqk:_Z_DEC]
    fz_ref[...] = zf


def _in_projection(x_all, mods, layer, g_pre, w_in2, wdec, bdec, qn, wuq, kvn, wukv, tab,
                   *, tm, cond_of_tile, tab_of_tile, kv_of_tile):
    t, d = x_all.shape
    nq = MLA_HEADS * MLA_SLOT
    row = lambda w: pl.BlockSpec((tm, w), lambda i: (i, 0))
    kv_row = pl.BlockSpec((tm, nq), lambda i: (kv_of_tile(i), 0))
    gla_row = pl.BlockSpec((tm, 2048), lambda i: (kv_of_tile(i), 0))
    return pl.pallas_call(
        _inproj_kernel,
        grid=(t // tm,),
        in_specs=[
            row(d),
            pl.BlockSpec((None, None, N_MOD, d), lambda i: (layer, cond_of_tile(i), 0, 0)),
            _const_spec((1, d)),
            _const_spec(w_in2.shape),
            _const_spec(wdec.shape),
            _const_spec(bdec.shape),
            _const_spec(qn.shape),
            _const_spec(wuq.shape),
            _const_spec(kvn.shape),
            _const_spec(wukv.shape),
            pl.BlockSpec((tm, 2 * LANES), lambda i: (tab_of_tile(i), 0)),
        ],
        out_specs=[gla_row, row(512), row(nq), kv_row, kv_row],
        out_shape=[
            jax.ShapeDtypeStruct((t, 2048), F32),
            jax.ShapeDtypeStruct((t, 512), F32),
            jax.ShapeDtypeStruct((t, nq), BF16),
            jax.ShapeDtypeStruct((t, nq), BF16),
            jax.ShapeDtypeStruct((t, nq), BF16),
        ],
        compiler_params=_params("parallel"),
        name="in_projection",
    )(x_all, mods, g_pre.reshape(1, d), w_in2, wdec, bdec, qn, wuq, kvn, wukv, tab)


def _gla_kernel(*refs, reverse, final):
    if final:
        g_ref, of_ref, gn_ref, o_ref, st_ref = refs
    else:
        g_ref, o_ref, st_ref = refs
    c = GLA_CHUNK
    n_batch, tb = g_ref.shape[0], g_ref.shape[1]
    nqk = GLA_HEADS * GLA_DK
    nv = GLA_HEADS * GLA_DV

    @pl.when(pl.program_id(0) == 0)
    def _():
        st_ref[...] = jnp.zeros(st_ref.shape, F32)

    row = lax.broadcasted_iota(jnp.int32, (c, c), 0)
    col = lax.broadcasted_iota(jnp.int32, (c, c), 1)
    keep = (col >= row) if reverse else (col <= row)
    lmat = jnp.where(keep, 1.0, 0.0).astype(BF16)
    la_off = _Z_DEC + (nqk if reverse else 0)
    mid = c // 2 if reverse else c // 2 - 1
    end = 0 if reverse else c - 1
    chunks = range(tb // c)
    order = list(reversed(chunks) if reverse else chunks)
    work = [(bi, slice(ci * c, (ci + 1) * c)) for bi in range(n_batch) for ci in order]
    heads = [(slice(hd * GLA_DK, (hd + 1) * GLA_DK), slice(hd * GLA_DV, (hd + 1) * GLA_DV))
             for hd in range(GLA_HEADS)]

    cums = []
    for bi, rows in work:
        la = g_ref[bi, rows, la_off:la_off + nqk]
        h1 = la.astype(BF16)
        r1 = la - h1.astype(F32)
        h2 = r1.astype(BF16)
        h3 = (r1 - h2.astype(F32)).astype(BF16)
        cums.append(_dot(lmat, h1) + _dot(lmat, h2) + _dot(lmat, h3))
    scaled = []
    for (bi, rows), b in zip(work, cums):
        ref = b[mid:mid + 1, :]
        tot = b[end:end + 1, :]
        qt = g_ref[bi, rows, 0:nqk] * jnp.exp(b - ref)
        kt = g_ref[bi, rows, nqk:2 * nqk] * jnp.exp(jnp.minimum(ref - b, GLA_EXP_CLAMP))
        q_in = (qt * jnp.exp(ref)).astype(BF16)
        k_st = (kt * jnp.exp(tot - ref)).astype(BF16)
        scaled.append((qt.astype(BF16), kt.astype(BF16), q_in, k_st, jnp.exp(tot)))
    scores = []
    for qt, kt, _, _, _ in scaled:
        scores.append([jnp.where(keep, lax.dot_general(qt[:, ks], kt[:, ks], NT, preferred_element_type=F32), 0.0)
                       .astype(BF16) for ks, _ in heads])
    intra, upd = [], []
    for (bi, rows), a, (_, _, _, k_st, _) in zip(work, scores, scaled):
        vals = [g_ref[bi, rows, 2 * nqk + vs.start:2 * nqk + vs.stop].astype(BF16) for _, vs in heads]
        intra.append([_dot(a[hd], vals[hd]) for hd in range(GLA_HEADS)])
        upd.append([lax.dot_general(vals[hd], k_st[:, heads[hd][0]], TN, preferred_element_type=F32)
                    for hd in range(GLA_HEADS)])
    states = {bi: [st_ref[bi, hd] for hd in range(GLA_HEADS)] for bi in range(n_batch)}
    for (bi, rows), o_in, u, (_, _, q_in, _, dec) in zip(work, intra, upd, scaled):
        for hd, (ks, vs) in enumerate(heads):
            st = states[bi][hd]
            o = o_in[hd] + lax.dot_general(q_in[:, ks], st.astype(BF16), NT, preferred_element_type=F32)
            states[bi][hd] = st * dec[:, ks] + u[hd]
            if final:
                o = o + of_ref[bi, rows, vs]
                gate = g_ref[bi, rows, 2 * nqk + nv + vs.start:2 * nqk + nv + vs.stop]
                o_ref[bi, rows, vs] = (_rms(o, gn_ref[...]) * _silu(gate)).astype(o_ref.dtype)
            else:
                o_ref[bi, rows, vs] = o
    for bi in range(n_batch):
        for hd in range(GLA_HEADS):
            st_ref[bi, hd] = states[bi][hd]


def _gla_scan(gla_in, o_fwd, g_norm, *, reverse, tb, seq, ctx):
    batch = gla_in.shape[0]
    n_lat, n_ctx = seq // tb, ctx // tb

    def blk(j):
        if reverse:
            return jnp.where(j < n_ctx, n_lat + (n_ctx - 1 - j), n_lat + n_ctx - 1 - j)
        return jnp.where(j < n_ctx, n_lat + j, j - n_ctx)

    final = o_fwd is not None
    dv = GLA_HEADS * GLA_DV
    in_specs = [pl.BlockSpec((batch, tb, gla_in.shape[2]), lambda j: (0, blk(j), 0))]
    args = [gla_in]
    if final:
        in_specs += [pl.BlockSpec((batch, tb, dv), lambda j: (0, blk(j), 0)), _const_spec((1, GLA_DV))]
        args += [o_fwd, g_norm.reshape(1, GLA_DV)]
    return pl.pallas_call(
        functools.partial(_gla_kernel, reverse=reverse, final=final),
        grid=(n_lat + n_ctx,),
        in_specs=in_specs,
        out_specs=pl.BlockSpec((batch, tb, dv), lambda j: (0, blk(j), 0)),
        out_shape=jax.ShapeDtypeStruct((batch, seq + ctx, dv), BF16 if final else F32),
        scratch_shapes=[pltpu.VMEM((batch, GLA_HEADS, GLA_DV, GLA_DK), F32)],
        compiler_params=_params("arbitrary"),
        name="gla_bwd_scan" if reverse else "gla_fwd_scan",
    )(*args)


def _dft_mats(n):
    k = np.arange(n)
    ang = (2.0 * np.pi / n) * ((k[:, None] * k[None, :]) % n)
    return np.cos(ang), np.sin(ang)


def _fft_kernel(x_ref, fa_ref, twc_ref, tws_ref, mb_ref, cs_ref, o_ref, zr_ref, zi_ref, *, na, nb, bg, kg, pitch):
    ch = x_ref.shape[1]
    fa = fa_ref[...].astype(BF16)
    for g0 in range(0, nb, bg * kg):
        starts = range(g0, min(g0 + bg * kg, nb), bg)
        xs = [jnp.concatenate([x_ref[pl.ds(b0 + i, na, stride=nb), :] for i in range(bg)], axis=1).astype(BF16)
              for b0 in starts]
        zs = [_dot(fa, x) for x in xs]
        for b0, z in zip(starts, zs):
            for i in range(bg):
                r0 = (b0 + i) * pitch
                zr_ref[r0:r0 + na, :] = z[:na, i * ch:(i + 1) * ch]
                zi_ref[r0:r0 + na, :] = z[na:, i * ch:(i + 1) * ch]
    mb = mb_ref[...].astype(BF16)
    cs = cs_ref[...].astype(BF16)

    for k0 in range(0, na, kg):
        group = range(k0, min(k0 + kg, na))
        ts = []
        for ka in group:
            zr, zi = zr_ref[pl.ds(ka, nb, stride=pitch), :], zi_ref[pl.ds(ka, nb, stride=pitch), :]
            c = twc_ref[ka * nb:(ka + 1) * nb, :]
            s = tws_ref[ka * nb:(ka + 1) * nb, :]
            ts.append(jnp.concatenate([zr * c + zi * s, zi * c - zr * s], axis=0).astype(BF16))
        vs = [_dot(mb, t) for t in ts]
        vvs = [jnp.concatenate([v[:nb], v[nb:]], axis=1).astype(BF16) for v in vs]
        outs = [_dot(vv, cs) for vv in vvs]
        for ka, out in zip(group, outs):
            o_ref[pl.ds(ka, nb, stride=na), :] = out


def _fourier_latent(fz, *, batch, seq):
    cw = fz.shape[1]
    na, nb = FFT_NA, seq // FFT_NA
    pitch = na + 1
    ch = FOURIER_CH
    ca, sa = _dft_mats(na)
    f_a = np.concatenate([ca, -sa], axis=0).astype(np.float32)
    cb, sb = _dft_mats(nb)
    m_b = np.concatenate([np.concatenate([cb, sb], axis=1),
                          np.concatenate([-sb, cb], axis=1)], axis=0).astype(np.float32)
    cc, sc = _dft_mats(ch)
    cs = (np.concatenate([cc, sc], axis=0) * ((seq * ch) ** -0.5)).astype(np.float32)
    ang = (2.0 * np.pi / seq) * ((np.arange(na)[:, None] * np.arange(nb)[None, :]) % seq)
    twc = jnp.broadcast_to(np.cos(ang).reshape(seq, 1).astype(np.float32), (seq, ch))
    tws = jnp.broadcast_to(np.sin(ang).reshape(seq, 1).astype(np.float32), (seq, ch))
    return pl.pallas_call(
        functools.partial(_fft_kernel, na=na, nb=nb, bg=4, kg=16, pitch=pitch),
        grid=(batch, cw // ch),
        in_specs=[pl.BlockSpec((seq, ch), lambda b, g: (b, g)), _const_spec(f_a.shape),
                  _const_spec((seq, ch)), _const_spec((seq, ch)), _const_spec(m_b.shape), _const_spec(cs.shape)],
        out_specs=pl.BlockSpec((seq, ch), lambda b, g: (b, g)),
        out_shape=jax.ShapeDtypeStruct((batch * seq, cw), F32),
        scratch_shapes=[pltpu.VMEM((nb * pitch, ch), F32), pltpu.VMEM((nb * pitch, ch), F32)],
        compiler_params=_params("parallel", "parallel"),
        name="fnet_latent",
    )(fz, f_a, twc, tws, m_b, cs)


def _fft_ctx_kernel(x_ref, f_ref, cs_ref, o_ref):
    ch = FOURIER_CH
    f = f_ref[...].astype(BF16)
    cs = cs_ref[...].astype(BF16)
    for g in range(x_ref.shape[1] // ch):
        ab = _dot(x_ref[:, g * ch:(g + 1) * ch].astype(BF16), cs)
        st = jnp.concatenate([ab[:, :ch], ab[:, ch:]], axis=0).astype(BF16)
        o_ref[:, g * ch:(g + 1) * ch] = _dot(f, st)


def _fourier_context(fz, *, batch, seq, ctx):
    cw = fz.shape[1]
    ch = FOURIER_CH
    cn, sn = _dft_mats(ctx)
    f = (np.concatenate([cn, -sn], axis=1) * ((ctx * ch) ** -0.5)).astype(np.float32)
    cc, sc = _dft_mats(ch)
    cs = np.concatenate([cc, sc], axis=1).astype(np.float32)
    blk0 = batch * seq // ctx
    return pl.pallas_call(
        _fft_ctx_kernel,
        grid=(batch,),
        in_specs=[
            pl.BlockSpec((ctx, cw), lambda b: (blk0 + b, 0)),
            _const_spec(f.shape),
            _const_spec(cs.shape),
        ],
        out_specs=pl.BlockSpec((ctx, cw), lambda b: (b, 0)),
        out_shape=jax.ShapeDtypeStruct((batch * ctx, cw), F32),
        compiler_params=_params("parallel"),
        name="fnet_context",
    )(fz, f, cs)


def _attn_update(s, v, carry):
    m, acc = carry
    m_new = jnp.maximum(m, jnp.max(s, axis=-1, keepdims=True))
    p = jnp.exp2(s - m_new).astype(BF16)
    acc = jnp.exp2(m - m_new) * acc + _dot(p, v)
    return m_new, acc


def _attn_kernel(q_ref, k_ref, v_ref, o_ref, *, kc):
    tq = q_ref.shape[0]
    sl = [slice(0, MLA_SLOT), slice(MLA_SLOT, 2 * MLA_SLOT)]
    qs = [q_ref[:, s] for s in sl]
    init = (jnp.full((tq, 1), -jnp.inf, F32), jnp.zeros((tq, MLA_SLOT), F32))
    carry = (init, init)
    bounds = [(c0, c0 + kc) for c0 in range(0, k_ref.shape[0], kc)]

    def scores(lo, hi):
        return [lax.dot_general(qs[h], k_ref[lo:hi, sl[h]], NT, preferred_element_type=F32) for h in range(2)]

    s_next = scores(*bounds[0])
    for i, (lo, hi) in enumerate(bounds):
        s_cur = s_next
        if i + 1 < len(bounds):
            s_next = scores(*bounds[i + 1])
        carry = tuple(_attn_update(s_cur[h], v_ref[lo:hi, sl[h]], carry[h]) for h in range(2))
    outs = [acc * (1.0 / acc[:, MLA_V:MLA_V + 1]) for (_, acc) in carry]
    lane = lax.broadcasted_iota(jnp.int32, outs[0].shape, 1)
    o_ref[...] = jnp.where(lane < MLA_V, outs[0], pltpu.roll(outs[1], MLA_V, 1)).astype(o_ref.dtype)


def _attention(q, k, v, *, latent, batch, seq, ctx, tq, kc):
    hp = MLA_HEADS // 2
    w = 2 * MLA_SLOT
    wv = 2 * MLA_V
    if latent:
        nq = seq // tq
        q_spec = pl.BlockSpec((tq, w), lambda b, h, i: (b * nq + i, h))
        kv_spec = pl.BlockSpec((seq + ctx, w), lambda b, h, i: (b, h))
    else:
        nq = ctx // tq
        q0 = batch * seq // tq
        per_batch = (seq + ctx) // ctx
        q_spec = pl.BlockSpec((tq, w), lambda b, h, i: (q0 + b * nq + i, h))
        kv_spec = pl.BlockSpec((ctx, w), lambda b, h, i: (b * per_batch + seq // ctx, h))
    return pl.pallas_call(
        functools.partial(_attn_kernel, kc=kc),
        grid=(batch, hp, nq),
        in_specs=[q_spec, kv_spec, kv_spec],
        out_specs=pl.BlockSpec((tq, wv), lambda b, h, i: (b * nq + i, h)),
        out_shape=jax.ShapeDtypeStruct((batch * nq * tq, hp * wv), BF16),
        compiler_params=_params("parallel", "parallel", "arbitrary"),
        name="mla_latent" if latent else "mla_context",
    )(q, k, v)


def _merge_kernel(*refs, n_lat_tiles, with_ctx, n_sub):
    x_ref, mod_ref, gpre_ref, gpost_ref, wgt_ref = refs[:5]
    ya_refs = refs[5:5 + n_sub]
    rest = refs[5 + n_sub:]
    if with_ctx:
        yb_ref, yc_ref, ybc_ref, ycc_ref, wb_ref, wo_ref, o_ref = rest
        is_ctx = pl.program_id(0) >= n_lat_tiles
    else:
        yb_ref, yc_ref, wb_ref, wo_ref, o_ref = rest
    d = x_ref.shape[1]
    sub = x_ref.shape[0] // n_sub
    xs, branches = [], []
    for j in range(n_sub):
        rows = slice(j * sub, (j + 1) * sub)
        yb, yc = yb_ref[rows, :], yc_ref[rows, :]
        if with_ctx:
            yb = jnp.where(is_ctx, ybc_ref[rows, :], yb)
            yc = jnp.where(is_ctx, ycc_ref[rows, :], yc)
        ys = (ya_refs[j][...], yb.astype(BF16), yc)
        branches.append([_dot(ys[br], wb_ref[br]) for br in range(3)])
        xs.append(x_ref[rows, :])
    hs = [_modulated(x, gpre_ref[...], mod_ref, 1).astype(BF16) for x in xs]
    gates = [[_dot(h, wgt_ref[:, br * d:(br + 1) * d]) for br in range(3)] for h in hs]
    ms = []
    for j in range(n_sub):
        m = jnp.zeros(xs[j].shape, F32)
        for br in range(3):
            m = m + jax.nn.sigmoid(gates[j][br]) * branches[j][br]
        ms.append(m.astype(BF16))
    outs = [_dot(m, wo_ref[...]) for m in ms]
    for j in range(n_sub):
        o_ref[j * sub:(j + 1) * sub, :] = xs[j] + mod_ref[5:6, :] * _rms(outs[j], gpost_ref[...])


def _merge(x_all, mods, layer, g_pre, g_post, w_gates, ya, yb, yc, yb_ctx, yc_ctx, wb, wo,
           *, n_rows, n_lat, tm, n_sub, cond_of_tile, kv_of_tile):
    d = x_all.shape[1]
    bw = ya.shape[1]
    with_ctx = yb_ctx is not None
    big = tm * n_sub
    n_lat_tiles = n_lat // big
    row = lambda w: pl.BlockSpec((big, w), lambda i: (i, 0))
    lat = lambda w: pl.BlockSpec((big, w), lambda i: (jnp.minimum(i, n_lat_tiles - 1), 0))
    ctx = lambda w: pl.BlockSpec((big, w), lambda i: (jnp.maximum(i - n_lat_tiles, 0), 0))
    ya_specs = [pl.BlockSpec((tm, bw), lambda i, j=j: (kv_of_tile(i * n_sub + j), 0)) for j in range(n_sub)]
    branch_specs = ya_specs + [lat(bw), lat(bw)] + ([ctx(bw), ctx(bw)] if with_ctx else [])
    branch_args = [ya] * n_sub + [yb, yc] + ([yb_ctx, yc_ctx] if with_ctx else [])
    return pl.pallas_call(
        functools.partial(_merge_kernel, n_lat_tiles=n_lat_tiles, with_ctx=with_ctx, n_sub=n_sub),
        grid=(n_rows // big,),
        in_specs=[
            row(d),
            pl.BlockSpec((None, None, N_MOD, d), lambda i: (layer, cond_of_tile(i * n_sub), 0, 0)),
            _const_spec((1, d)),
            _const_spec((1, d)),
            _const_spec(w_gates.shape),
            *branch_specs,
            pl.BlockSpec((None,) + wb.shape[1:], lambda i: (layer, 0, 0, 0), pipeline_mode=pl.Buffered(1)),
            pl.BlockSpec((None,) + wo.shape[1:], lambda i: (layer, 0, 0), pipeline_mode=pl.Buffered(1)),
        ],
        out_specs=row(d),
        out_shape=jax.ShapeDtypeStruct((n_rows, d), F32),
        compiler_params=_params("parallel"),
        name="merge",
    )(x_all, mods, g_pre.reshape(1, d), g_post.reshape(1, d), w_gates, *branch_args, wb, wo)


def _rope_table(seq, tm):
    pos = np.arange(seq)
    half = MLA_ROPE // 2
    inv_freq = ROPE_BASE ** (-np.arange(0, half, 2) / half)
    ar, ac = (pos // GRID_W)[:, None] * inv_freq, (pos % GRID_W)[:, None] * inv_freq
    cos = np.concatenate([np.cos(ar), np.cos(ar), np.cos(ac), np.cos(ac)], axis=1)
    sin = np.concatenate([-np.sin(ar), np.sin(ar), -np.sin(ac), np.sin(ac)], axis=1)
    pad = LANES - MLA_NOPE - MLA_ROPE
    rows = lambda n, c, s: np.concatenate(
        [np.ones((n, MLA_NOPE)), c, np.zeros((n, pad)), np.zeros((n, MLA_NOPE)), s, np.zeros((n, pad))], axis=1)
    tab = rows(seq, cos, sin)
    ident = rows(tm, np.ones((tm, MLA_ROPE)), np.zeros((tm, MLA_ROPE)))
    return np.concatenate([tab, ident], axis=0).astype(np.float32)


def _with_rope_partners(w):
    j = np.arange(MLA_ROPE)
    h = MLA_ROPE // 4
    partner = np.where((j % (2 * h)) < h, j + h, j - h)
    rope = w[..., -MLA_ROPE:]
    return jnp.concatenate([w, rope[..., partner]], axis=-1)


def _pad_cols(w, n):
    return jnp.pad(w, ((0, 0), (0, n - w.shape[1])))


def _layer_weights(layer, w_in, gla_w_decay, gla_b_decay, mla_w_uq, mla_w_ukv):
    nqk = GLA_HEADS * GLA_DK
    r = GLA_GATE_RANK
    o_dec = 2 * nqk + 2 * GLA_HEADS * GLA_DV
    o_four = o_dec + 2 * r
    o_cq = o_four + FOURIER_GROUPS * FOURIER_CH
    q_rank = mla_w_uq.shape[1]
    kv_rank = mla_w_ukv.shape[1]
    o_ckv = o_cq + q_rank
    o_kr = o_ckv + kv_rank
    o_gates = o_kr + MLA_ROPE
    w = w_in[layer]
    w_in2 = jnp.concatenate([
        w[:, :o_dec], _pad_cols(w[:, o_dec:o_four], LANES), w[:, o_four:o_cq], w[:, o_cq:o_ckv],
        w[:, o_ckv:o_kr], _pad_cols(_with_rope_partners(w[:, o_kr:o_gates]), LANES)], axis=1).astype(BF16)
    w_gates = w[:, o_gates:].astype(BF16)
    wd = gla_w_decay[layer]
    wdec = jnp.zeros((LANES, 2 * nqk), F32)
    wdec = wdec.at[0:r, 0:nqk].set(wd[0]).at[r:2 * r, nqk:].set(wd[1]).astype(BF16)
    bdec = gla_b_decay[layer].reshape(1, 2 * nqk)
    dq = MLA_NOPE + MLA_ROPE
    wuq = _with_rope_partners(mla_w_uq[layer].reshape(q_rank, MLA_HEADS, dq))
    wuq = wuq.reshape(q_rank, MLA_HEADS * MLA_SLOT).astype(BF16)
    wkv = mla_w_ukv[layer].reshape(kv_rank, MLA_HEADS, MLA_NOPE + MLA_V)
    wk = jnp.pad(wkv[:, :, :MLA_NOPE], ((0, 0), (0, 0), (0, MLA_SLOT - MLA_NOPE)))
    wv = jnp.pad(wkv[:, :, MLA_NOPE:], ((0, 0), (0, 0), (0, MLA_SLOT - MLA_V)))
    wukv = jnp.concatenate([wk.reshape(kv_rank, MLA_HEADS * MLA_SLOT),
                            wv.reshape(kv_rank, MLA_HEADS * MLA_SLOT)], axis=1).astype(BF16)
    return w_in2, w_gates, wdec, bdec, wuq, wukv


def kernel(x, c, ctx, c_ctx, w_mod, b_mod, norm_pre, norm_post, ffn_w_gate, ffn_w_up, ffn_w_down, w_in,
           gla_w_decay, gla_b_decay, gla_norm, mla_q_norm, mla_w_uq, mla_kv_norm, mla_w_ukv, w_branch, w_out):
    batch, seq, d = x.shape
    n_ctx = ctx.shape[1]
    depth = w_mod.shape[0]
    n_lat = batch * seq
    t = n_lat + batch * n_ctx
    tm = min(256, n_ctx)
    tm_ffn = FFN_TM if seq % FFN_TM == 0 and (batch * n_ctx) % FFN_TM == 0 else tm
    d_ff = ffn_w_gate.shape[-1]
    fc = d_ff // 2 if (d_ff // 2) % LANES == 0 else d_ff
    assert seq % tm == 0 and n_ctx % tm == 0 and seq % (FFT_NA * 8) == 0 and batch < MOD_ROWS

    cond_of_tile = lambda i: jnp.minimum((i * tm) // seq, batch)
    tab_of_tile = lambda i: jnp.where(i * tm < n_lat, ((i * tm) % seq) // tm, seq // tm)
    lat_tiles, ctx_tiles = seq // tm, n_ctx // tm

    def kv_of_tile(i):
        j = i - batch * lat_tiles
        lat = (i // lat_tiles) * (lat_tiles + ctx_tiles) + i % lat_tiles
        ctx_row = (j // ctx_tiles) * (lat_tiles + ctx_tiles) + lat_tiles + j % ctx_tiles
        return jnp.where(j < 0, lat, ctx_row)

    n_keys = seq + n_ctx
    key_chunk = max(kc for kc in range(LANES, min(ATTN_KC, n_keys) + 1, LANES) if n_keys % kc == 0)

    cond = jnp.concatenate([c, c_ctx[None, :], jnp.zeros((MOD_ROWS - batch - 1, d), F32)], axis=0)
    mods = _modulation(cond, w_mod, b_mod, tn=(N_MOD * d) // 8).reshape(depth, MOD_ROWS, N_MOD, d)
    tab = _rope_table(seq, tm)
    x_all, x_ctx = x.reshape(n_lat, d), ctx.reshape(batch * n_ctx, d)
    wg_all, wu_all, wd_all = ffn_w_gate.astype(BF16), ffn_w_up.astype(BF16), ffn_w_down.astype(BF16)
    wb_all, wo_all = w_branch.astype(BF16), w_out.astype(BF16)

    for layer in range(depth):
        last = layer == depth - 1
        ffn = lambda xa, xc, sub, s, rows: _half_ffn(
            xa, xc, mods, layer, sub, s, norm_pre[layer, sub], norm_post[layer, sub], wg_all, wu_all, wd_all,
            n_rows=rows, tm=tm_ffn, cond_of_tile=lambda i: jnp.minimum((i * tm_ffn) // seq, batch), fc=fc)
        x_all = ffn(x_all, x_ctx if layer == 0 else None, 0, 0, t)

        w_in2, w_gates, wdec, bdec, wuq, wukv = _layer_weights(
            layer, w_in, gla_w_decay, gla_b_decay, mla_w_uq, mla_w_ukv)
        gla_in, fz, q, k, v = _in_projection(
            x_all, mods, layer, norm_pre[layer, 1], w_in2, wdec, bdec,
            mla_q_norm[layer].reshape(1, -1), wuq, mla_kv_norm[layer].reshape(1, -1), wukv, tab,
            tm=tm, cond_of_tile=cond_of_tile, tab_of_tile=tab_of_tile, kv_of_tile=kv_of_tile)

        scan = functools.partial(_gla_scan, tb=tm, seq=seq, ctx=n_ctx)
        gla_in = gla_in.reshape(batch, n_keys, gla_in.shape[1])
        o_fwd = scan(gla_in, None, None, reverse=False)
        ya = scan(gla_in, o_fwd, gla_norm[layer], reverse=True).reshape(batch * n_keys, -1)

        rows_out = n_lat if last else t
        yb = _fourier_latent(fz, batch=batch, seq=seq)
        yc = _attention(q, k, v, latent=True, batch=batch, seq=seq, ctx=n_ctx,
                        tq=min(ATTN_TQ, seq), kc=key_chunk)
        yb_ctx = yc_ctx = None
        if not last:
            yb_ctx = _fourier_context(fz, batch=batch, seq=seq, ctx=n_ctx)
            yc_ctx = _attention(q, k, v, latent=False, batch=batch, seq=seq, ctx=n_ctx, tq=tm, kc=n_ctx)

        x_all = _merge(x_all, mods, layer, norm_pre[layer, 1], norm_post[layer, 1], w_gates, ya, yb, yc,
                       yb_ctx, yc_ctx, wb_all, wo_all,
                       n_rows=rows_out, n_lat=n_lat, tm=tm, n_sub=tm_ffn // tm, cond_of_tile=cond_of_tile,
                       kv_of_tile=kv_of_tile)
        x_all = ffn(x_all, None, 2, 1, rows_out)
    return x_all.reshape(batch, seq, d)
```

```python
import functools

import jax
import jax.numpy as jnp
import numpy as np
from jax import lax
from jax.experimental import pallas as pl
from jax.experimental.pallas import tpu as pltpu

F32 = jnp.float32
BF16 = jnp.bfloat16

GRID_W = 64
N_MOD = 9
GLA_HEADS = 4
GLA_DK = 64
GLA_DV = 128
GLA_GATE_RANK = 16
GLA_TAU = 16.0
GLA_CHUNK = 64
GLA_EXP_CLAMP = 80.0
FOURIER_GROUPS = 4
FOURIER_CH = 128
FFT_NA = 64
MLA_HEADS = 8
MLA_NOPE = 64
MLA_ROPE = 32
MLA_V = 64
MLA_SLOT = 128
ROPE_BASE = 10000.0
EPS = 1e-6
LOG2_E = 1.4426950408889634
FFN_TM = 512
ATTN_TQ = 512
ATTN_KC = 4096
LANES = 128
MOD_ROWS = 8
VMEM_LIMIT = 56 * 1024 * 1024

NT = (((1,), (1,)), ((), ()))
TN = (((0,), (0,)), ((), ()))


def _params(*sem):
    return pltpu.CompilerParams(dimension_semantics=sem, vmem_limit_bytes=VMEM_LIMIT)


def _const_spec(shape):
    nd = len(shape)
    return pl.BlockSpec(shape, lambda *_: (0,) * nd, pipeline_mode=pl.Buffered(1))


def _dot(a, b):
    return jnp.dot(a, b, preferred_element_type=F32)


def _rms(x, g):
    return x * lax.rsqrt(jnp.mean(x * x, axis=-1, keepdims=True) + EPS) * g


def _silu(x):
    return x * jax.nn.sigmoid(x)


def _modulated(x, g, mod_ref, sub):
    shift = mod_ref[3 * sub + 0:3 * sub + 1, :]
    scale = mod_ref[3 * sub + 1:3 * sub + 2, :]
    return _rms(x, g) * (1.0 + scale) + shift


def _mod_kernel(c_ref, w_ref, b_ref, o_ref):
    a = _silu(c_ref[...]).astype(BF16)
    o_ref[0] = _dot(a, w_ref[0].astype(BF16)) + b_ref[0]


def _modulation(cond, w_mod, b_mod, tn):
    depth, d, nm = w_mod.shape
    return pl.pallas_call(
        _mod_kernel,
        grid=(depth, nm // tn),
        in_specs=[
            pl.BlockSpec((MOD_ROWS, d), lambda l, j: (0, 0)),
            pl.BlockSpec((1, d, tn), lambda l, j: (l, 0, j)),
            pl.BlockSpec((1, 1, tn), lambda l, j: (l, 0, j)),
        ],
        out_specs=pl.BlockSpec((1, MOD_ROWS, tn), lambda l, j: (l, 0, j)),
        out_shape=jax.ShapeDtypeStruct((depth, MOD_ROWS, nm), F32),
        compiler_params=_params("parallel", "parallel"),
        name="modulation",
    )(cond, w_mod, b_mod.reshape(depth, 1, nm))


def _ffn_kernel(*refs, sub, fc, n_lat_tiles):
    if n_lat_tiles is None:
        x_ref, mod_ref, gpre_ref, gpost_ref, wg_ref, wu_ref, wd_ref, o_ref = refs
        xc_ref = None
    else:
        x_ref, xc_ref, mod_ref, gpre_ref, gpost_ref, wg_ref, wu_ref, wd_ref, o_ref = refs
    d_ff = wg_ref.shape[1]
    gate = mod_ref[3 * sub + 2:3 * sub + 3, :]
    tm = x_ref.shape[0]
    half = tm // 2 if tm >= 512 else tm
    xs, hs = [], []
    for r0 in range(0, tm, half):
        rows = slice(r0, r0 + half)
        x = x_ref[rows, :]
        if xc_ref is not None:
            x = jnp.where(pl.program_id(0) >= n_lat_tiles, xc_ref[rows, :], x)
        xs.append(x)
        hs.append(_modulated(x, gpre_ref[...], mod_ref, sub).astype(BF16))
    ys = [jnp.zeros(x.shape, F32) for x in xs]
    chunks = list(range(0, d_ff, fc))
    up = lambda c0: ([_dot(h, wg_ref[:, c0:c0 + fc]) for h in hs], [_dot(h, wu_ref[:, c0:c0 + fc]) for h in hs])
    nxt = up(chunks[0])
    for i, c0 in enumerate(chunks):
        gs, us = nxt
        if i + 1 < len(chunks):
            nxt = up(chunks[i + 1])
        acts = [(_silu(g) * u).astype(BF16) for g, u in zip(gs, us)]
        ys = [y + _dot(a, wd_ref[c0:c0 + fc, :]) for y, a in zip(ys, acts)]
    for i, (x, y) in enumerate(zip(xs, ys)):
        o_ref[i * half:(i + 1) * half, :] = x + 0.5 * gate * _rms(y, gpost_ref[...])


def _half_ffn(x_all, x_ctx, mods, layer, sub, ffn_idx, g_pre, g_post, wg, wu, wd,
              *, n_rows, tm, cond_of_tile, fc):
    d = x_all.shape[1]
    d_ff = wg.shape[-1]
    stacked = lambda r, c: pl.BlockSpec((None, None, r, c), lambda i: (layer, ffn_idx, 0, 0),
                                        pipeline_mode=pl.Buffered(1))
    if x_ctx is None:
        n_lat_tiles = None
        x_specs, xs = [pl.BlockSpec((tm, d), lambda i: (i, 0))], [x_all]
    else:
        n_lat_tiles = x_all.shape[0] // tm
        x_specs = [pl.BlockSpec((tm, d), lambda i: (jnp.minimum(i, n_lat_tiles - 1), 0)),
                   pl.BlockSpec((tm, d), lambda i: (jnp.maximum(i - n_lat_tiles, 0), 0))]
        xs = [x_all, x_ctx]
    return pl.pallas_call(
        functools.partial(_ffn_kernel, sub=sub, fc=fc, n_lat_tiles=n_lat_tiles),
        grid=(n_rows // tm,),
        in_specs=[
            *x_specs,
            pl.BlockSpec((None, None, N_MOD, d), lambda i: (layer, cond_of_tile(i), 0, 0)),
            _const_spec((1, d)),
            _const_spec((1, d)),
            stacked(d, d_ff),
            stacked(d, d_ff),
            stacked(d_ff, d),
        ],
        out_specs=pl.BlockSpec((tm, d), lambda i: (i, 0)),
        out_shape=jax.ShapeDtypeStruct((n_rows, d), F32),
        compiler_params=_params("parallel"),
        name=f"half_ffn_{sub}",
    )(*xs, mods, g_pre.reshape(1, d), g_post.reshape(1, d), wg, wu, wd)


_Z_GLA = 0
_Z_DEC = 1536
_Z_FOUR = 1664
_Z_CQ = 2176
_Z_CKV = 2560
_Z_KR = 2816
_Z_END = 2944


def _rope_slot(x, tab):
    return x * tab[:, :LANES] + pltpu.roll(x, LANES - MLA_ROPE, 1) * tab[:, LANES:]


def _inproj_kernel(x_ref, mod_ref, gpre_ref, win_ref, wdec_ref, bdec_ref, qn_ref, wuq_ref,
                   kvn_ref, wukv_ref, tab_ref, gla_ref, fz_ref, q_ref, k_ref, v_ref):
    h = _modulated(x_ref[...], gpre_ref[...], mod_ref, 1).astype(BF16)
    zm = _dot(h, win_ref[:, _Z_CQ:_Z_END])
    z = _dot(h, win_ref[:, 0:_Z_CQ])
    cq = _rms(zm[:, 0:_Z_CKV - _Z_CQ], qn_ref[...]).astype(BF16)
    ckv = _rms(zm[:, _Z_CKV - _Z_CQ:_Z_KR - _Z_CQ], kvn_ref[...]).astype(BF16)
    q = _dot(cq, wuq_ref[...])
    kv = _dot(ckv, wukv_ref[...])
    nqk = GLA_HEADS * GLA_DK
    gla_ref[:, 0:nqk] = z[:, 0:nqk] * (GLA_DK ** -0.5)
    gla_ref[:, nqk:_Z_DEC] = z[:, nqk:_Z_DEC]
    xd = _dot(z[:, _Z_DEC:_Z_FOUR].astype(BF16), wdec_ref[...]) + bdec_ref[...]
    logsig = jnp.minimum(xd, 0.0) - jnp.log1p(jnp.exp(-jnp.abs(xd)))
    gla_ref[:, _Z_DEC:_Z_DEC + 2 * nqk] = logsig * (1.0 / GLA_TAU)
    fz_ref[...] = z[:, _Z_FOUR:_Z_CQ]
    tab = tab_ref[...]
    nk = MLA_HEADS * MLA_SLOT
    kr = _rope_slot(pltpu.roll(zm[:, _Z_KR - _Z_CQ:], MLA_NOPE, 1), tab)
    scale = (MLA_NOPE + MLA_ROPE) ** -0.5 * LOG2_E
    for hd in range(MLA_HEADS):
        sl = slice(hd * MLA_SLOT, (hd + 1) * MLA_SLOT)
        q_ref[:, sl] = (_rope_slot(q[:, sl], tab) * scale).astype(BF16)
        k_ref[:, sl] = (kv[:, sl] + kr).astype(BF16)
    lane = lax.broadcasted_iota(jnp.int32, (x_ref.shape[0], nk), 1)
    v_ref[...] = jnp.where(lane % MLA_SLOT == MLA_V, 1.0, kv[:, nk:]).astype(BF16)


def _in_projection(x_all, mods, layer, g_pre, w_in2, wdec, bdec, qn, wuq, kvn, wukv, tab,
                   *, tm, cond_of_tile, tab_of_tile, kv_of_tile):
    t, d = x_all.shape
    nq = MLA_HEADS * MLA_SLOT
    row = lambda w: pl.BlockSpec((tm, w), lambda i: (i, 0))
    kv_row = pl.BlockSpec((tm, nq), lambda i: (kv_of_tile(i), 0))
    gla_row = pl.BlockSpec((tm, 2048), lambda i: (kv_of_tile(i), 0))
    return pl.pallas_call(
        _inproj_kernel,
        grid=(t // tm,),
        in_specs=[
            row(d),
            pl.BlockSpec((None, None, N_MOD, d), lambda i: (layer, cond_of_tile(i), 0, 0)),
            _const_spec((1, d)),
            _const_spec(w_in2.shape),
            _const_spec(wdec.shape),
            _const_spec(bdec.shape),
            _const_spec(qn.shape),
            _const_spec(wuq.shape),
            _const_spec(kvn.shape),
            _const_spec(wukv.shape),
            pl.BlockSpec((tm, 2 * LANES), lambda i: (tab_of_tile(i), 0)),
        ],
        out_specs=[gla_row, row(512), row(nq), kv_row, kv_row],
        out_shape=[
            jax.ShapeDtypeStruct((t, 2048), F32),
            jax.ShapeDtypeStruct((t, 512), F32),
            jax.ShapeDtypeStruct((t, nq), BF16),
            jax.ShapeDtypeStruct((t, nq), BF16),
            jax.ShapeDtypeStruct((t, nq), BF16),
        ],
        compiler_params=_params("parallel"),
        name="in_projection",
    )(x_all, mods, g_pre.reshape(1, d), w_in2, wdec, bdec, qn, wuq, kvn, wukv, tab)


def _gla_kernel(*refs, reverse, final):
    if final:
        g_ref, of_ref, gn_ref, o_ref, st_ref = refs
    else:
        g_ref, o_ref, st_ref = refs
    c = GLA_CHUNK
    n_batch, tb = g_ref.shape[0], g_ref.shape[1]
    nqk = GLA_HEADS * GLA_DK
    nv = GLA_HEADS * GLA_DV

    @pl.when(pl.program_id(0) == 0)
    def _():
        st_ref[...] = jnp.zeros(st_ref.shape, F32)

    row = lax.broadcasted_iota(jnp.int32, (c, c), 0)
    col = lax.broadcasted_iota(jnp.int32, (c, c), 1)
    keep = (col >= row) if reverse else (col <= row)
    lmat = jnp.where(keep, 1.0, 0.0).astype(BF16)
    la_off = _Z_DEC + (nqk if reverse else 0)
    mid = c // 2 if reverse else c // 2 - 1
    end = 0 if reverse else c - 1
    chunks = range(tb // c)
    order = list(reversed(chunks) if reverse else chunks)
    work = [(bi, slice(ci * c, (ci + 1) * c)) for bi in range(n_batch) for ci in order]
    heads = [(slice(hd * GLA_DK, (hd + 1) * GLA_DK), slice(hd * GLA_DV, (hd + 1) * GLA_DV))
             for hd in range(GLA_HEADS)]

    cums = []
    for bi, rows in work:
        la = g_ref[bi, rows, la_off:la_off + nqk]
        h1 = la.astype(BF16)
        r1 = la - h1.astype(F32)
        h2 = r1.astype(BF16)
        h3 = (r1 - h2.astype(F32)).astype(BF16)
        cums.append(_dot(lmat, h1) + _dot(lmat, h2) + _dot(lmat, h3))
    scaled = []
    for (bi, rows), b in zip(work, cums):
        ref = b[mid:mid + 1, :]
        tot = b[end:end + 1, :]
        qt = g_ref[bi, rows, 0:nqk] * jnp.exp(b - ref)
        kt = g_ref[bi, rows, nqk:2 * nqk] * jnp.exp(jnp.minimum(ref - b, GLA_EXP_CLAMP))
        q_in = (qt * jnp.exp(ref)).astype(BF16)
        k_st = (kt * jnp.exp(tot - ref)).astype(BF16)
        scaled.append((qt.astype(BF16), kt.astype(BF16), q_in, k_st, jnp.exp(tot)))
    scores = []
    for qt, kt, _, _, _ in scaled:
        scores.append([jnp.where(keep, lax.dot_general(qt[:, ks], kt[:, ks], NT, preferred_element_type=F32), 0.0)
                       .astype(BF16) for ks, _ in heads])
    intra, upd = [], []
    for (bi, rows), a, (_, _, _, k_st, _) in zip(work, scores, scaled):
        vals = [g_ref[bi, rows, 2 * nqk + vs.start:2 * nqk + vs.stop].astype(BF16) for _, vs in heads]
        intra.append([_dot(a[hd], vals[hd]) for hd in range(GLA_HEADS)])
        upd.append([lax.dot_general(vals[hd], k_st[:, heads[hd][0]], TN, preferred_element_type=F32)
                    for hd in range(GLA_HEADS)])
    states = {bi: [st_ref[bi, hd] for hd in range(GLA_HEADS)] for bi in range(n_batch)}
    for (bi, rows), o_in, u, (_, _, q_in, _, dec) in zip(work, intra, upd, scaled):
        for hd, (ks, vs) in enumerate(heads):
            st = states[bi][hd]
            o = o_in[hd] + lax.dot_general(q_in[:, ks], st.astype(BF16), NT, preferred_element_type=F32)
            states[bi][hd] = st * dec[:, ks] + u[hd]
            if final:
                o = o + of_ref[bi, rows, vs]
                gate = g_ref[bi, rows, 2 * nqk + nv + vs.start:2 * nqk + nv + vs.stop]
                o_ref[bi, rows, vs] = (_rms(o, gn_ref[...]) * _silu(gate)).astype(o_ref.dtype)
            else:
                o_ref[bi, rows, vs] = o
    for bi in range(n_batch):
        for hd in range(GLA_HEADS):
            st_ref[bi, hd] = states[bi][hd]


def _gla_scan(gla_in, o_fwd, g_norm, *, reverse, tb, seq, ctx):
    batch = gla_in.shape[0]
    n_lat, n_ctx = seq // tb, ctx // tb

    def blk(j):
        if reverse:
            return jnp.where(j < n_ctx, n_lat + (n_ctx - 1 - j), n_lat + n_ctx - 1 - j)
        return jnp.where(j < n_ctx, n_lat + j, j - n_ctx)

    final = o_fwd is not None
    dv = GLA_HEADS * GLA_DV
    in_specs = [pl.BlockSpec((batch, tb, gla_in.shape[2]), lambda j: (0, blk(j), 0))]
    args = [gla_in]
    if final:
        in_specs += [pl.BlockSpec((batch, tb, dv), lambda j: (0, blk(j), 0)), _const_spec((1, GLA_DV))]
        args += [o_fwd, g_norm.reshape(1, GLA_DV)]
    return pl.pallas_call(
        functools.partial(_gla_kernel, reverse=reverse, final=final),
        grid=(n_lat + n_ctx,),
        in_specs=in_specs,
        out_specs=pl.BlockSpec((batch, tb, dv), lambda j: (0, blk(j), 0)),
        out_shape=jax.ShapeDtypeStruct((batch, seq + ctx, dv), BF16 if final else F32),
        scratch_shapes=[pltpu.VMEM((batch, GLA_HEADS, GLA_DV, GLA_DK), F32)],
        compiler_params=_params("arbitrary"),
        name="gla_bwd_scan" if reverse else "gla_fwd_scan",
    )(*args)


def _dft_mats(n):
    k = np.arange(n)
    ang = (2.0 * np.pi / n) * ((k[:, None] * k[None, :]) % n)
    return np.cos(ang), np.sin(ang)


def _fft_kernel(x_ref, fa_ref, twc_ref, tws_ref, mb_ref, cs_ref, o_ref, zr_ref, zi_ref, *, na, nb, bg, kg, pitch):
    ch = x_ref.shape[1]
    fa = fa_ref[...].astype(BF16)
    for g0 in range(0, nb, bg * kg):
        starts = range(g0, min(g0 + bg * kg, nb), bg)
        xs = [jnp.concatenate([x_ref[pl.ds(b0 + i, na, stride=nb), :] for i in range(bg)], axis=1).astype(BF16)
              for b0 in starts]
        zs = [_dot(fa, x) for x in xs]
        for b0, z in zip(starts, zs):
            for i in range(bg):
                r0 = (b0 + i) * pitch
                zr_ref[r0:r0 + na, :] = z[:na, i * ch:(i + 1) * ch]
                zi_ref[r0:r0 + na, :] = z[na:, i * ch:(i + 1) * ch]
    mb = mb_ref[...].astype(BF16)
    cs = cs_ref[...].astype(BF16)

    for k0 in range(0, na, kg):
        group = range(k0, min(k0 + kg, na))
        ts = []
        for ka in group:
            zr, zi = zr_ref[pl.ds(ka, nb, stride=pitch), :], zi_ref[pl.ds(ka, nb, stride=pitch), :]
            c = twc_ref[ka * nb:(ka + 1) * nb, :]
            s = tws_ref[ka * nb:(ka + 1) * nb, :]
            ts.append(jnp.concatenate([zr * c + zi * s, zi * c - zr * s], axis=0).astype(BF16))
        vs = [_dot(mb, t) for t in ts]
        vvs = [jnp.concatenate([v[:nb], v[nb:]], axis=1).astype(BF16) for v in vs]
        outs = [_dot(vv, cs) for vv in vvs]
        for ka, out in zip(group, outs):
            o_ref[pl.ds(ka, nb, stride=na), :] = out


def _fourier_latent(fz, *, batch, seq):
    cw = fz.shape[1]
    na, nb = FFT_NA, seq // FFT_NA
    pitch = na + 1
    ch = FOURIER_CH
    ca, sa = _dft_mats(na)
    f_a = np.concatenate([ca, -sa], axis=0).astype(np.float32)
    cb, sb = _dft_mats(nb)
    m_b = np.concatenate([np.concatenate([cb, sb], axis=1),
                          np.concatenate([-sb, cb], axis=1)], axis=0).astype(np.float32)
    cc, sc = _dft_mats(ch)
    cs = (np.concatenate([cc, sc], axis=0) * ((seq * ch) ** -0.5)).astype(np.float32)
    ang = (2.0 * np.pi / seq) * ((np.arange(na)[:, None] * np.arange(nb)[None, :]) % seq)
    twc = jnp.broadcast_to(np.cos(ang).reshape(seq, 1).astype(np.float32), (seq, ch))
    tws = jnp.broadcast_to(np.sin(ang).reshape(seq, 1).astype(np.float32), (seq, ch))
    return pl.pallas_call(
        functools.partial(_fft_kernel, na=na, nb=nb, bg=4, kg=16, pitch=pitch),
        grid=(batch, cw // ch),
        in_specs=[pl.BlockSpec((seq, ch), lambda b, g: (b, g)), _const_spec(f_a.shape),
                  _const_spec((seq, ch)), _const_spec((seq, ch)), _const_spec(m_b.shape), _const_spec(cs.shape)],
        out_specs=pl.BlockSpec((seq, ch), lambda b, g: (b, g)),
        out_shape=jax.ShapeDtypeStruct((batch * seq, cw), F32),
        scratch_shapes=[pltpu.VMEM((nb * pitch, ch), F32), pltpu.VMEM((nb * pitch, ch), F32)],
        compiler_params=_params("parallel", "parallel"),
        name="fnet_latent",
    )(fz, f_a, twc, tws, m_b, cs)


def _fft_ctx_kernel(x_ref, f_ref, cs_ref, o_ref):
    ch = FOURIER_CH
    f = f_ref[...].astype(BF16)
    cs = cs_ref[...].astype(BF16)
    for g in range(x_ref.shape[1] // ch):
        ab = _dot(x_ref[:, g * ch:(g + 1) * ch].astype(BF16), cs)
        st = jnp.concatenate([ab[:, :ch], ab[:, ch:]], axis=0).astype(BF16)
        o_ref[:, g * ch:(g + 1) * ch] = _dot(f, st)


def _fourier_context(fz, *, batch, seq, ctx):
    cw = fz.shape[1]
    ch = FOURIER_CH
    cn, sn = _dft_mats(ctx)
    f = (np.concatenate([cn, -sn], axis=1) * ((ctx * ch) ** -0.5)).astype(np.float32)
    cc, sc = _dft_mats(ch)
    cs = np.concatenate([cc, sc], axis=1).astype(np.float32)
    blk0 = batch * seq // ctx
    return pl.pallas_call(
        _fft_ctx_kernel,
        grid=(batch,),
        in_specs=[
            pl.BlockSpec((ctx, cw), lambda b: (blk0 + b, 0)),
            _const_spec(f.shape),
            _const_spec(cs.shape),
        ],
        out_specs=pl.BlockSpec((ctx, cw), lambda b: (b, 0)),
        out_shape=jax.ShapeDtypeStruct((batch * ctx, cw), F32),
        compiler_params=_params("parallel"),
        name="fnet_context",
    )(fz, f, cs)


def _attn_update(s, v, carry):
    m, acc = carry
    m_new = jnp.maximum(m, jnp.max(s, axis=-1, keepdims=True))
    p = jnp.exp2(s - m_new).astype(BF16)
    acc = jnp.exp2(m - m_new) * acc + _dot(p, v)
    return m_new, acc


def _attn_kernel(q_ref, k_ref, v_ref, o_ref, *, kc):
    tq = q_ref.shape[0]
    sl = [slice(0, MLA_SLOT), slice(MLA_SLOT, 2 * MLA_SLOT)]
    qs = [q_ref[:, s] for s in sl]
    init = (jnp.full((tq, 1), -jnp.inf, F32), jnp.zeros((tq, MLA_SLOT), F32))
    carry = (init, init)
    bounds = [(c0, c0 + kc) for c0 in range(0, k_ref.shape[0], kc)]

    def scores(lo, hi):
        return [lax.dot_general(qs[h], k_ref[lo:hi, sl[h]], NT, preferred_element_type=F32) for h in range(2)]

    s_next = scores(*bounds[0])
    for i, (lo, hi) in enumerate(bounds):
        s_cur = s_next
        if i + 1 < len(bounds):
            s_next = scores(*bounds[i + 1])
        carry = tuple(_attn_update(s_cur[h], v_ref[lo:hi, sl[h]], carry[h]) for h in range(2))
    outs = [acc * (1.0 / acc[:, MLA_V:MLA_V + 1]) for (_, acc) in carry]
    lane = lax.broadcasted_iota(jnp.int32, outs[0].shape, 1)
    o_ref[...] = jnp.where(lane < MLA_V, outs[0], pltpu.roll(outs[1], MLA_V, 1)).astype(o_ref.dtype)


def _attention(q, k, v, *, latent, batch, seq, ctx, tq, kc):
    hp = MLA_HEADS // 2
    w = 2 * MLA_SLOT
    wv = 2 * MLA_V
    if latent:
        nq = seq // tq
        q_spec = pl.BlockSpec((tq, w), lambda b, h, i: (b * nq + i, h))
        kv_spec = pl.BlockSpec((seq + ctx, w), lambda b, h, i: (b, h))
    else:
        nq = ctx // tq
        q0 = batch * seq // tq
        per_batch = (seq + ctx) // ctx
        q_spec = pl.BlockSpec((tq, w), lambda b, h, i: (q0 + b * nq + i, h))
        kv_spec = pl.BlockSpec((ctx, w), lambda b, h, i: (b * per_batch + seq // ctx, h))
    return pl.pallas_call(
        functools.partial(_attn_kernel, kc=kc),
        grid=(batch, hp, nq),
        in_specs=[q_spec, kv_spec, kv_spec],
        out_specs=pl.BlockSpec((tq, wv), lambda b, h, i: (b * nq + i, h)),
        out_shape=jax.ShapeDtypeStruct((batch * nq * tq, hp * wv), BF16),
        compiler_params=_params("parallel", "parallel", "arbitrary"),
        name="mla_latent" if latent else "mla_context",
    )(q, k, v)


def _merge_kernel(*refs, n_lat_tiles, with_ctx, n_sub):
    x_ref, mod_ref, gpre_ref, gpost_ref, wgt_ref = refs[:5]
    ya_refs = refs[5:5 + n_sub]
    rest = refs[5 + n_sub:]
    if with_ctx:
        yb_ref, yc_ref, ybc_ref, ycc_ref, wb_ref, wo_ref, o_ref = rest
        is_ctx = pl.program_id(0) >= n_lat_tiles
    else:
        yb_ref, yc_ref, wb_ref, wo_ref, o_ref = rest
    d = x_ref.shape[1]
    sub = x_ref.shape[0] // n_sub
    xs, branches = [], []
    for j in range(n_sub):
        rows = slice(j * sub, (j + 1) * sub)
        yb, yc = yb_ref[rows, :], yc_ref[rows, :]
        if with_ctx:
            yb = jnp.where(is_ctx, ybc_ref[rows, :], yb)
            yc = jnp.where(is_ctx, ycc_ref[rows, :], yc)
        ys = (ya_refs[j][...], yb.astype(BF16), yc)
        branches.append([_dot(ys[br], wb_ref[br]) for br in range(3)])
        xs.append(x_ref[rows, :])
    hs = [_modulated(x, gpre_ref[...], mod_ref, 1).astype(BF16) for x in xs]
    gates = [[_dot(h, wgt_ref[:, br * d:(br + 1) * d]) for br in range(3)] for h in hs]
    ms = []
    for j in range(n_sub):
        m = jnp.zeros(xs[j].shape, F32)
        for br in range(3):
            m = m + jax.nn.sigmoid(gates[j][br]) * branches[j][br]
        ms.append(m.astype(BF16))
    outs = [_dot(m, wo_ref[...]) for m in ms]
    for j in range(n_sub):
        o_ref[j * sub:(j + 1) * sub, :] = xs[j] + mod_ref[5:6, :] * _rms(outs[j], gpost_ref[...])


def _merge(x_all, mods, layer, g_pre, g_post, w_gates, ya, yb, yc, yb_ctx, yc_ctx, wb, wo,
           *, n_rows, n_lat, tm, n_sub, cond_of_tile, kv_of_tile):
    d = x_all.shape[1]
    bw = ya.shape[1]
    with_ctx = yb_ctx is not None
    big = tm * n_sub
    n_lat_tiles = n_lat // big
    row = lambda w: pl.BlockSpec((big, w), lambda i: (i, 0))
    lat = lambda w: pl.BlockSpec((big, w), lambda i: (jnp.minimum(i, n_lat_tiles - 1), 0))
    ctx = lambda w: pl.BlockSpec((big, w), lambda i: (jnp.maximum(i - n_lat_tiles, 0), 0))
    ya_specs = [pl.BlockSpec((tm, bw), lambda i, j=j: (kv_of_tile(i * n_sub + j), 0)) for j in range(n_sub)]
    branch_specs = ya_specs + [lat(bw), lat(bw)] + ([ctx(bw), ctx(bw)] if with_ctx else [])
    branch_args = [ya] * n_sub + [yb, yc] + ([yb_ctx, yc_ctx] if with_ctx else [])
    return pl.pallas_call(
        functools.partial(_merge_kernel, n_lat_tiles=n_lat_tiles, with_ctx=with_ctx, n_sub=n_sub),
        grid=(n_rows // big,),
        in_specs=[
            row(d),
            pl.BlockSpec((None, None, N_MOD, d), lambda i: (layer, cond_of_tile(i * n_sub), 0, 0)),
            _const_spec((1, d)),
            _const_spec((1, d)),
            _const_spec(w_gates.shape),
            *branch_specs,
            pl.BlockSpec((None,) + wb.shape[1:], lambda i: (layer, 0, 0, 0), pipeline_mode=pl.Buffered(1)),
            pl.BlockSpec((None,) + wo.shape[1:], lambda i: (layer, 0, 0), pipeline_mode=pl.Buffered(1)),
        ],
        out_specs=row(d),
        out_shape=jax.ShapeDtypeStruct((n_rows, d), F32),
        compiler_params=_params("parallel"),
        name="merge",
    )(x_all, mods, g_pre.reshape(1, d), g_post.reshape(1, d), w_gates, *branch_args, wb, wo)


def _rope_table(seq, tm):
    pos = np.arange(seq)
    half = MLA_ROPE // 2
    inv_freq = ROPE_BASE ** (-np.arange(0, half, 2) / half)
    ar, ac = (pos // GRID_W)[:, None] * inv_freq, (pos % GRID_W)[:, None] * inv_freq
    cos = np.concatenate([np.cos(ar), np.cos(ar), np.cos(ac), np.cos(ac)], axis=1)
    sin = np.concatenate([-np.sin(ar), np.sin(ar), -np.sin(ac), np.sin(ac)], axis=1)
    pad = LANES - MLA_NOPE - MLA_ROPE
    rows = lambda n, c, s: np.concatenate(
        [np.ones((n, MLA_NOPE)), c, np.zeros((n, pad)), np.zeros((n, MLA_NOPE)), s, np.zeros((n, pad))], axis=1)
    tab = rows(seq, cos, sin)
    ident = rows(tm, np.ones((tm, MLA_ROPE)), np.zeros((tm, MLA_ROPE)))
    return np.concatenate([tab, ident], axis=0).astype(np.float32)


def _with_rope_partners(w):
    j = np.arange(MLA_ROPE)
    h = MLA_ROPE // 4
    partner = np.where((j % (2 * h)) < h, j + h, j - h)
    rope = w[..., -MLA_ROPE:]
    return jnp.concatenate([w, rope[..., partner]], axis=-1)


def _pad_cols(w, n):
    return jnp.pad(w, ((0, 0), (0, n - w.shape[1])))


def _layer_weights(layer, w_in, gla_w_decay, gla_b_decay, mla_w_uq, mla_w_ukv):
    nqk = GLA_HEADS * GLA_DK
    r = GLA_GATE_RANK
    o_dec = 2 * nqk + 2 * GLA_HEADS * GLA_DV
    o_four = o_dec + 2 * r
    o_cq = o_four + FOURIER_GROUPS * FOURIER_CH
    q_rank = mla_w_uq.shape[1]
    kv_rank = mla_w_ukv.shape[1]
    o_ckv = o_cq + q_rank
    o_kr = o_ckv + kv_rank
    o_gates = o_kr + MLA_ROPE
    w = w_in[layer]
    w_in2 = jnp.concatenate([
        w[:, :o_dec], _pad_cols(w[:, o_dec:o_four], LANES), w[:, o_four:o_cq], w[:, o_cq:o_ckv],
        w[:, o_ckv:o_kr], _pad_cols(_with_rope_partners(w[:, o_kr:o_gates]), LANES)], axis=1).astype(BF16)
    w_gates = w[:, o_gates:].astype(BF16)
    wd = gla_w_decay[layer]
    wdec = jnp.zeros((LANES, 2 * nqk), F32)
    wdec = wdec.at[0:r, 0:nqk].set(wd[0]).at[r:2 * r, nqk:].set(wd[1]).astype(BF16)
    bdec = gla_b_decay[layer].reshape(1, 2 * nqk)
    dq = MLA_NOPE + MLA_ROPE
    wuq = _with_rope_partners(mla_w_uq[layer].reshape(q_rank, MLA_HEADS, dq))
    wuq = wuq.reshape(q_rank, MLA_HEADS * MLA_SLOT).astype(BF16)
    wkv = mla_w_ukv[layer].reshape(kv_rank, MLA_HEADS, MLA_NOPE + MLA_V)
    wk = jnp.pad(wkv[:, :, :MLA_NOPE], ((0, 0), (0, 0), (0, MLA_SLOT - MLA_NOPE)))
    wv = jnp.pad(wkv[:, :, MLA_NOPE:], ((0, 0), (0, 0), (0, MLA_SLOT - MLA_V)))
    wukv = jnp.concatenate([wk.reshape(kv_rank, MLA_HEADS * MLA_SLOT),
                            wv.reshape(kv_rank, MLA_HEADS * MLA_SLOT)], axis=1).astype(BF16)
    return w_in2, w_gates, wdec, bdec, wuq, wukv


def kernel(x, c, ctx, c_ctx, w_mod, b_mod, norm_pre, norm_post, ffn_w_gate, ffn_w_up, ffn_w_down, w_in,
           gla_w_decay, gla_b_decay, gla_norm, mla_q_norm, mla_w_uq, mla_kv_norm, mla_w_ukv, w_branch, w_out):
    batch, seq, d = x.shape
    n_ctx = ctx.shape[1]
    depth = w_mod.shape[0]
    n_lat = batch * seq
    t = n_lat + batch * n_ctx
    tm = min(256, n_ctx)
    tm_ffn = FFN_TM if seq % FFN_TM == 0 and (batch * n_ctx) % FFN_TM == 0 else tm
    d_ff = ffn_w_gate.shape[-1]
    fc = 256 if d_ff % 256 == 0 else d_ff
    assert seq % tm == 0 and n_ctx % tm == 0 and seq % (FFT_NA * 8) == 0 and batch < MOD_ROWS

    cond_of_tile = lambda i: jnp.minimum((i * tm) // seq, batch)
    tab_of_tile = lambda i: jnp.where(i * tm < n_lat, ((i * tm) % seq) // tm, seq // tm)
    lat_tiles, ctx_tiles = seq // tm, n_ctx // tm

    def kv_of_tile(i):
        j = i - batch * lat_tiles
        lat = (i // lat_tiles) * (lat_tiles + ctx_tiles) + i % lat_tiles
        ctx_row = (j // ctx_tiles) * (lat_tiles + ctx_tiles) + lat_tiles + j % ctx_tiles
        return jnp.where(j < 0, lat, ctx_row)

    n_keys = seq + n_ctx
    key_chunk = max(kc for kc in range(LANES, min(ATTN_KC, n_keys) + 1, LANES) if n_keys % kc == 0)

    cond = jnp.concatenate([c, c_ctx[None, :], jnp.zeros((MOD_ROWS - batch - 1, d), F32)], axis=0)
    mods = _modulation(cond, w_mod, b_mod, tn=(N_MOD * d) // 8).reshape(depth, MOD_ROWS, N_MOD, d)
    tab = _rope_table(seq, tm)
    x_all, x_ctx = x.reshape(n_lat, d), ctx.reshape(batch * n_ctx, d)
    wg_all, wu_all, wd_all = ffn_w_gate.astype(BF16), ffn_w_up.astype(BF16), ffn_w_down.astype(BF16)
    wb_all, wo_all = w_branch.astype(BF16), w_out.astype(BF16)

    for layer in range(depth):
        last = layer == depth - 1
        ffn = lambda xa, xc, sub, s, rows: _half_ffn(
            xa, xc, mods, layer, sub, s, norm_pre[layer, sub], norm_post[layer, sub], wg_all, wu_all, wd_all,
            n_rows=rows, tm=tm_ffn, cond_of_tile=lambda i: jnp.minimum((i * tm_ffn) // seq, batch), fc=fc)
        x_all = ffn(x_all, x_ctx if layer == 0 else None, 0, 0, t)

        w_in2, w_gates, wdec, bdec, wuq, wukv = _layer_weights(
            layer, w_in, gla_w_decay, gla_b_decay, mla_w_uq, mla_w_ukv)
        gla_in, fz, q, k, v = _in_projection(
            x_all, mods, layer, norm_pre[layer, 1], w_in2, wdec, bdec,
            mla_q_norm[layer].reshape(1, -1), wuq, mla_kv_norm[layer].reshape(1, -1), wukv, tab,
            tm=tm, cond_of_tile=cond_of_tile, tab_of_tile=tab_of_tile, kv_of_tile=kv_of_tile)

        scan = functools.partial(_gla_scan, tb=tm, seq=seq, ctx=n_ctx)
        gla_in = gla_in.reshape(batch, n_keys, gla_in.shape[1])
        o_fwd = scan(gla_in, None, None, reverse=False)
        ya = scan(gla_in, o_fwd, gla_norm[layer], reverse=True).reshape(batch * n_keys, -1)

        rows_out = n_lat if last else t
        yb = _fourier_latent(fz, batch=batch, seq=seq)
        yc = _attention(q, k, v, latent=True, batch=batch, seq=seq, ctx=n_ctx,
                        tq=min(ATTN_TQ, seq), kc=key_chunk)
        yb_ctx = yc_ctx = None
        if not last:
            yb_ctx = _fourier_context(fz, batch=batch, seq=seq, ctx=n_ctx)
            yc_ctx = _attention(q, k, v, latent=False, batch=batch, seq=seq, ctx=n_ctx, tq=tm, kc=n_ctx)

        x_all = _merge(x_all, mods, layer, norm_pre[layer, 1], norm_post[layer, 1], w_gates, ya, yb, yc,
                       yb_ctx, yc_ctx, wb_all, wo_all,
                       n_rows=rows_out, n_lat=n_lat, tm=tm, n_sub=tm_ffn // tm, cond_of_tile=cond_of_tile,
                       kv_of_tile=kv_of_tile)
        x_all = ffn(x_all, None, 2, 1, rows_out)
    return x_all.reshape(batch, seq, d)
```

```python
import functools

import jax
import jax.numpy as jnp
import numpy as np
from jax import lax
from jax.experimental import pallas as pl
from jax.experimental.pallas import tpu as pltpu

F32 = jnp.float32
BF16 = jnp.bfloat16

GRID_W = 64
N_MOD = 9
GLA_HEADS = 4
GLA_DK = 64
GLA_DV = 128
GLA_GATE_RANK = 16
GLA_TAU = 16.0
GLA_CHUNK = 64
GLA_EXP_CLAMP = 80.0
FOURIER_GROUPS = 4
FOURIER_CH = 128
FFT_NA = 64
MLA_HEADS = 8
MLA_NOPE = 64
MLA_ROPE = 32
MLA_V = 64
MLA_SLOT = 128
ROPE_BASE = 10000.0
EPS = 1e-6
LOG2_E = 1.4426950408889634
FFN_TM = 512
ATTN_TQ = 512
ATTN_KC = 4096
LANES = 128
MOD_ROWS = 8
VMEM_LIMIT = 56 * 1024 * 1024

NT = (((1,), (1,)), ((), ()))
TN = (((0,), (0,)), ((), ()))


def _params(*sem):
    return pltpu.CompilerParams(dimension_semantics=sem, vmem_limit_bytes=VMEM_LIMIT)


def _const_spec(shape):
    nd = len(shape)
    return pl.BlockSpec(shape, lambda *_: (0,) * nd, pipeline_mode=pl.Buffered(1))


def _dot(a, b):
    return jnp.dot(a, b, preferred_element_type=F32)


def _rms(x, g):
    return x * lax.rsqrt(jnp.mean(x * x, axis=-1, keepdims=True) + EPS) * g


def _silu(x):
    return x * jax.nn.sigmoid(x)


def _modulated(x, g, mod_ref, sub):
    shift = mod_ref[3 * sub + 0:3 * sub + 1, :]
    scale = mod_ref[3 * sub + 1:3 * sub + 2, :]
    return _rms(x, g) * (1.0 + scale) + shift


def _mod_kernel(c_ref, w_ref, b_ref, o_ref):
    o_ref[0] = _dot(_silu(c_ref[...]), w_ref[0]) + b_ref[0]


def _modulation(cond, w_mod, b_mod, tn):
    depth, d, nm = w_mod.shape
    return pl.pallas_call(
        _mod_kernel,
        grid=(depth, nm // tn),
        in_specs=[
            pl.BlockSpec((MOD_ROWS, d), lambda l, j: (0, 0)),
            pl.BlockSpec((1, d, tn), lambda l, j: (l, 0, j)),
            pl.BlockSpec((1, 1, tn), lambda l, j: (l, 0, j)),
        ],
        out_specs=pl.BlockSpec((1, MOD_ROWS, tn), lambda l, j: (l, 0, j)),
        out_shape=jax.ShapeDtypeStruct((depth, MOD_ROWS, nm), F32),
        compiler_params=_params("parallel", "parallel"),
        name="modulation",
    )(cond, w_mod, b_mod.reshape(depth, 1, nm))


def _ffn_kernel(*refs, sub, fc, n_lat_tiles):
    if n_lat_tiles is None:
        x_ref, mod_ref, gpre_ref, gpost_ref, wg_ref, wu_ref, wd_ref, o_ref = refs
        xc_ref = None
    else:
        x_ref, xc_ref, mod_ref, gpre_ref, gpost_ref, wg_ref, wu_ref, wd_ref, o_ref = refs
    d_ff = wg_ref.shape[1]
    gate = mod_ref[3 * sub + 2:3 * sub + 3, :]
    tm = x_ref.shape[0]
    half = tm // 2 if tm >= 512 else tm
    xs, hs = [], []
    for r0 in range(0, tm, half):
        rows = slice(r0, r0 + half)
        x = x_ref[rows, :]
        if xc_ref is not None:
            x = jnp.where(pl.program_id(0) >= n_lat_tiles, xc_ref[rows, :], x)
        xs.append(x)
        hs.append(_modulated(x, gpre_ref[...], mod_ref, sub).astype(BF16))
    ys = [jnp.zeros(x.shape, F32) for x in xs]
    chunks = list(range(0, d_ff, fc))
    up = lambda c0: ([_dot(h, wg_ref[:, c0:c0 + fc]) for h in hs], [_dot(h, wu_ref[:, c0:c0 + fc]) for h in hs])
    nxt = up(chunks[0])
    for i, c0 in enumerate(chunks):
        gs, us = nxt
        if i + 1 < len(chunks):
            nxt = up(chunks[i + 1])
        acts = [(_silu(g) * u).astype(BF16) for g, u in zip(gs, us)]
        ys = [y + _dot(a, wd_ref[c0:c0 + fc, :]) for y, a in zip(ys, acts)]
    for i, (x, y) in enumerate(zip(xs, ys)):
        o_ref[i * half:(i + 1) * half, :] = x + 0.5 * gate * _rms(y, gpost_ref[...])


def _half_ffn(x_all, x_ctx, mods, layer, sub, ffn_idx, g_pre, g_post, wg, wu, wd,
              *, n_rows, tm, cond_of_tile, fc):
    d = x_all.shape[1]
    d_ff = wg.shape[-1]
    stacked = lambda r, c: pl.BlockSpec((None, None, r, c), lambda i: (layer, ffn_idx, 0, 0),
                                        pipeline_mode=pl.Buffered(1))
    if x_ctx is None:
        n_lat_tiles = None
        x_specs, xs = [pl.BlockSpec((tm, d), lambda i: (i, 0))], [x_all]
    else:
        n_lat_tiles = x_all.shape[0] // tm
        x_specs = [pl.BlockSpec((tm, d), lambda i: (jnp.minimum(i, n_lat_tiles - 1), 0)),
                   pl.BlockSpec((tm, d), lambda i: (jnp.maximum(i - n_lat_tiles, 0), 0))]
        xs = [x_all, x_ctx]
    return pl.pallas_call(
        functools.partial(_ffn_kernel, sub=sub, fc=fc, n_lat_tiles=n_lat_tiles),
        grid=(n_rows // tm,),
        in_specs=[
            *x_specs,
            pl.BlockSpec((None, None, N_MOD, d), lambda i: (layer, cond_of_tile(i), 0, 0)),
            _const_spec((1, d)),
            _const_spec((1, d)),
            stacked(d, d_ff),
            stacked(d, d_ff),
            stacked(d_ff, d),
        ],
        out_specs=pl.BlockSpec((tm, d), lambda i: (i, 0)),
        out_shape=jax.ShapeDtypeStruct((n_rows, d), F32),
        compiler_params=_params("parallel"),
        name=f"half_ffn_{sub}",
    )(*xs, mods, g_pre.reshape(1, d), g_post.reshape(1, d), wg, wu, wd)


_Z_GLA = 0
_Z_DEC = 1536
_Z_FOUR = 1664
_Z_CQ = 2176
_Z_CKV = 2560
_Z_KR = 2816
_Z_END = 2944


def _rope_slot(x, tab):
    return x * tab[:, :LANES] + pltpu.roll(x, LANES - MLA_ROPE, 1) * tab[:, LANES:]


def _inproj_kernel(x_ref, mod_ref, gpre_ref, win_ref, wdec_ref, bdec_ref, qn_ref, wuq_ref,
                   kvn_ref, wukv_ref, tab_ref, gla_ref, fz_ref, q_ref, k_ref, v_ref):
    h = _modulated(x_ref[...], gpre_ref[...], mod_ref, 1).astype(BF16)
    zm = _dot(h, win_ref[:, _Z_CQ:_Z_END])
    z = _dot(h, win_ref[:, 0:_Z_CQ])
    cq = _rms(zm[:, 0:_Z_CKV - _Z_CQ], qn_ref[...]).astype(BF16)
    ckv = _rms(zm[:, _Z_CKV - _Z_CQ:_Z_KR - _Z_CQ], kvn_ref[...]).astype(BF16)
    q = _dot(cq, wuq_ref[...])
    kv = _dot(ckv, wukv_ref[...])
    nqk = GLA_HEADS * GLA_DK
    gla_ref[:, 0:nqk] = z[:, 0:nqk] * (GLA_DK ** -0.5)
    gla_ref[:, nqk:_Z_DEC] = z[:, nqk:_Z_DEC]
    xd = _dot(z[:, _Z_DEC:_Z_FOUR].astype(BF16), wdec_ref[...]) + bdec_ref[...]
    logsig = jnp.minimum(xd, 0.0) - jnp.log1p(jnp.exp(-jnp.abs(xd)))
    gla_ref[:, _Z_DEC:_Z_DEC + 2 * nqk] = logsig * (1.0 / GLA_TAU)
    fz_ref[...] = z[:, _Z_FOUR:_Z_CQ]
    tab = tab_ref[...]
    nk = MLA_HEADS * MLA_SLOT
    kr = _rope_slot(pltpu.roll(zm[:, _Z_KR - _Z_CQ:], MLA_NOPE, 1), tab)
    scale = (MLA_NOPE + MLA_ROPE) ** -0.5 * LOG2_E
    for hd in range(MLA_HEADS):
        sl = slice(hd * MLA_SLOT, (hd + 1) * MLA_SLOT)
        q_ref[:, sl] = (_rope_slot(q[:, sl], tab) * scale).astype(BF16)
        k_ref[:, sl] = (kv[:, sl] + kr).astype(BF16)
    lane = lax.broadcasted_iota(jnp.int32, (x_ref.shape[0], nk), 1)
    v_ref[...] = jnp.where(lane % MLA_SLOT == MLA_V, 1.0, kv[:, nk:]).astype(BF16)


def _in_projection(x_all, mods, layer, g_pre, w_in2, wdec, bdec, qn, wuq, kvn, wukv, tab,
                   *, tm, cond_of_tile, tab_of_tile, kv_of_tile):
    t, d = x_all.shape
    nq = MLA_HEADS * MLA_SLOT
    row = lambda w: pl.BlockSpec((tm, w), lambda i: (i, 0))
    kv_row = pl.BlockSpec((tm, nq), lambda i: (kv_of_tile(i), 0))
    gla_row = pl.BlockSpec((tm, 2048), lambda i: (kv_of_tile(i), 0))
    return pl.pallas_call(
        _inproj_kernel,
        grid=(t // tm,),
        in_specs=[
            row(d),
            pl.BlockSpec((None, None, N_MOD, d), lambda i: (layer, cond_of_tile(i), 0, 0)),
            _const_spec((1, d)),
            _const_spec(w_in2.shape),
            _const_spec(wdec.shape),
            _const_spec(bdec.shape),
            _const_spec(qn.shape),
            _const_spec(wuq.shape),
            _const_spec(kvn.shape),
            _const_spec(wukv.shape),
            pl.BlockSpec((tm, 2 * LANES), lambda i: (tab_of_tile(i), 0)),
        ],
        out_specs=[gla_row, row(512), row(nq), kv_row, kv_row],
        out_shape=[
            jax.ShapeDtypeStruct((t, 2048), F32),
            jax.ShapeDtypeStruct((t, 512), F32),
            jax.ShapeDtypeStruct((t, nq), BF16),
            jax.ShapeDtypeStruct((t, nq), BF16),
            jax.ShapeDtypeStruct((t, nq), BF16),
        ],
        compiler_params=_params("parallel"),
        name="in_projection",
    )(x_all, mods, g_pre.reshape(1, d), w_in2, wdec, bdec, qn, wuq, kvn, wukv, tab)


def _gla_kernel(*refs, reverse, final):
    if final:
        g_ref, of_ref, gn_ref, o_ref, st_ref = refs
    else:
        g_ref, o_ref, st_ref = refs
    c = GLA_CHUNK
    n_batch, tb = g_ref.shape[0], g_ref.shape[1]
    nqk = GLA_HEADS * GLA_DK
    nv = GLA_HEADS * GLA_DV

    @pl.when(pl.program_id(0) == 0)
    def _():
        st_ref[...] = jnp.zeros(st_ref.shape, F32)

    row = lax.broadcasted_iota(jnp.int32, (c, c), 0)
    col = lax.broadcasted_iota(jnp.int32, (c, c), 1)
    keep = (col >= row) if reverse else (col <= row)
    lmat = jnp.where(keep, 1.0, 0.0).astype(BF16)
    la_off = _Z_DEC + (nqk if reverse else 0)
    mid = c // 2 if reverse else c // 2 - 1
    end = 0 if reverse else c - 1
    chunks = range(tb // c)
    order = list(reversed(chunks) if reverse else chunks)
    work = [(bi, slice(ci * c, (ci + 1) * c)) for bi in range(n_batch) for ci in order]
    heads = [(slice(hd * GLA_DK, (hd + 1) * GLA_DK), slice(hd * GLA_DV, (hd + 1) * GLA_DV))
             for hd in range(GLA_HEADS)]

    cums = []
    for bi, rows in work:
        la = g_ref[bi, rows, la_off:la_off + nqk]
        h1 = la.astype(BF16)
        r1 = la - h1.astype(F32)
        h2 = r1.astype(BF16)
        h3 = (r1 - h2.astype(F32)).astype(BF16)
        cums.append(_dot(lmat, h1) + _dot(lmat, h2) + _dot(lmat, h3))
    scaled = []
    for (bi, rows), b in zip(work, cums):
        ref = b[mid:mid + 1, :]
        tot = b[end:end + 1, :]
        qt = g_ref[bi, rows, 0:nqk] * jnp.exp(b - ref)
        kt = g_ref[bi, rows, nqk:2 * nqk] * jnp.exp(jnp.minimum(ref - b, GLA_EXP_CLAMP))
        q_in = (qt * jnp.exp(ref)).astype(BF16)
        k_st = (kt * jnp.exp(tot - ref)).astype(BF16)
        scaled.append((qt.astype(BF16), kt.astype(BF16), q_in, k_st, jnp.exp(tot)))
    scores = []
    for qt, kt, _, _, _ in scaled:
        scores.append([jnp.where(keep, lax.dot_general(qt[:, ks], kt[:, ks], NT, preferred_element_type=F32), 0.0)
                       .astype(BF16) for ks, _ in heads])
    intra, upd = [], []
    for (bi, rows), a, (_, _, _, k_st, _) in zip(work, scores, scaled):
        vals = [g_ref[bi, rows, 2 * nqk + vs.start:2 * nqk + vs.stop].astype(BF16) for _, vs in heads]
        intra.append([_dot(a[hd], vals[hd]) for hd in range(GLA_HEADS)])
        upd.append([lax.dot_general(vals[hd], k_st[:, heads[hd][0]], TN, preferred_element_type=F32)
                    for hd in range(GLA_HEADS)])
    states = {bi: [st_ref[bi, hd] for hd in range(GLA_HEADS)] for bi in range(n_batch)}
    for (bi, rows), o_in, u, (_, _, q_in, _, dec) in zip(work, intra, upd, scaled):
        for hd, (ks, vs) in enumerate(heads):
            st = states[bi][hd]
            o = o_in[hd] + lax.dot_general(q_in[:, ks], st.astype(BF16), NT, preferred_element_type=F32)
            states[bi][hd] = st * dec[:, ks] + u[hd]
            if final:
                o = o + of_ref[bi, rows, vs]
                gate = g_ref[bi, rows, 2 * nqk + nv + vs.start:2 * nqk + nv + vs.stop]
                o_ref[bi, rows, vs] = (_rms(o, gn_ref[...]) * _silu(gate)).astype(o_ref.dtype)
            else:
                o_ref[bi, rows, vs] = o
    for bi in range(n_batch):
        for hd in range(GLA_HEADS):
            st_ref[bi, hd] = states[bi][hd]


def _gla_scan(gla_in, o_fwd, g_norm, *, reverse, tb, seq, ctx):
    batch = gla_in.shape[0]
    n_lat, n_ctx = seq // tb, ctx // tb

    def blk(j):
        if reverse:
            return jnp.where(j < n_ctx, n_lat + (n_ctx - 1 - j), n_lat + n_ctx - 1 - j)
        return jnp.where(j < n_ctx, n_lat + j, j - n_ctx)

    final = o_fwd is not None
    dv = GLA_HEADS * GLA_DV
    in_specs = [pl.BlockSpec((batch, tb, gla_in.shape[2]), lambda j: (0, blk(j), 0))]
    args = [gla_in]
    if final:
        in_specs += [pl.BlockSpec((batch, tb, dv), lambda j: (0, blk(j), 0)), _const_spec((1, GLA_DV))]
        args += [o_fwd, g_norm.reshape(1, GLA_DV)]
    return pl.pallas_call(
        functools.partial(_gla_kernel, reverse=reverse, final=final),
        grid=(n_lat + n_ctx,),
        in_specs=in_specs,
        out_specs=pl.BlockSpec((batch, tb, dv), lambda j: (0, blk(j), 0)),
        out_shape=jax.ShapeDtypeStruct((batch, seq + ctx, dv), BF16 if final else F32),
        scratch_shapes=[pltpu.VMEM((batch, GLA_HEADS, GLA_DV, GLA_DK), F32)],
        compiler_params=_params("arbitrary"),
        name="gla_bwd_scan" if reverse else "gla_fwd_scan",
    )(*args)


def _dft_mats(n):
    k = np.arange(n)
    ang = (2.0 * np.pi / n) * ((k[:, None] * k[None, :]) % n)
    return np.cos(ang), np.sin(ang)


def _fft_kernel(x_ref, fa_ref, twc_ref, tws_ref, mb_ref, cs_ref, o_ref, xt_ref, zr_ref, zi_ref,
                *, na, nb, bg, kg, pitch):
    ch = x_ref.shape[1]
    fa = fa_ref[...].astype(BF16)
    for a in range(na):
        xt_ref[pl.ds(a, nb, stride=pitch), :] = x_ref[a * nb:(a + 1) * nb, :]
    for g0 in range(0, nb, bg * kg):
        starts = range(g0, min(g0 + bg * kg, nb), bg)
        xs = [jnp.concatenate([xt_ref[(b0 + i) * pitch:(b0 + i) * pitch + na, :] for i in range(bg)],
                              axis=1).astype(BF16) for b0 in starts]
        zs = [_dot(fa, x) for x in xs]
        for b0, z in zip(starts, zs):
            for i in range(bg):
                r0 = (b0 + i) * pitch
                zr_ref[r0:r0 + na, :] = z[:na, i * ch:(i + 1) * ch]
                zi_ref[r0:r0 + na, :] = z[na:, i * ch:(i + 1) * ch]
    mb = mb_ref[...].astype(BF16)
    cs = cs_ref[...].astype(BF16)

    for k0 in range(0, na, kg):
        group = range(k0, min(k0 + kg, na))
        ts = []
        for ka in group:
            zr, zi = zr_ref[pl.ds(ka, nb, stride=pitch), :], zi_ref[pl.ds(ka, nb, stride=pitch), :]
            c = twc_ref[ka * nb:(ka + 1) * nb, :]
            s = tws_ref[ka * nb:(ka + 1) * nb, :]
            ts.append(jnp.concatenate([zr * c + zi * s, zi * c - zr * s], axis=0).astype(BF16))
        vs = [_dot(mb, t) for t in ts]
        vvs = [jnp.concatenate([v[:nb], v[nb:]], axis=1).astype(BF16) for v in vs]
        outs = [_dot(vv, cs) for vv in vvs]
        for ka, out in zip(group, outs):
            o_ref[pl.ds(ka, nb, stride=na), :] = out


def _fourier_latent(fz, *, batch, seq):
    cw = fz.shape[1]
    na, nb = FFT_NA, seq // FFT_NA
    pitch = na + 1
    ch = FOURIER_CH
    ca, sa = _dft_mats(na)
    f_a = np.concatenate([ca, -sa], axis=0).astype(np.float32)
    cb, sb = _dft_mats(nb)
    m_b = np.concatenate([np.concatenate([cb, sb], axis=1),
                          np.concatenate([-sb, cb], axis=1)], axis=0).astype(np.float32)
    cc, sc = _dft_mats(ch)
    cs = (np.concatenate([cc, sc], axis=0) * ((seq * ch) ** -0.5)).astype(np.float32)
    ang = (2.0 * np.pi / seq) * ((np.arange(na)[:, None] * np.arange(nb)[None, :]) % seq)
    twc = jnp.broadcast_to(np.cos(ang).reshape(seq, 1).astype(np.float32), (seq, ch))
    tws = jnp.broadcast_to(np.sin(ang).reshape(seq, 1).astype(np.float32), (seq, ch))
    return pl.pallas_call(
        functools.partial(_fft_kernel, na=na, nb=nb, bg=4, kg=16, pitch=pitch),
        grid=(batch, cw // ch),
        in_specs=[pl.BlockSpec((seq, ch), lambda b, g: (b, g)), _const_spec(f_a.shape),
                  _const_spec((seq, ch)), _const_spec((seq, ch)), _const_spec(m_b.shape), _const_spec(cs.shape)],
        out_specs=pl.BlockSpec((seq, ch), lambda b, g: (b, g)),
        out_shape=jax.ShapeDtypeStruct((batch * seq, cw), F32),
        scratch_shapes=[pltpu.VMEM((nb * pitch, ch), F32)] * 3,
        compiler_params=_params("parallel", "parallel"),
        name="fnet_latent",
    )(fz, f_a, twc, tws, m_b, cs)


def _fft_ctx_kernel(x_ref, f_ref, cs_ref, o_ref):
    ch = FOURIER_CH
    f = f_ref[...].astype(BF16)
    cs = cs_ref[...].astype(BF16)
    for g in range(x_ref.shape[1] // ch):
        ab = _dot(x_ref[:, g * ch:(g + 1) * ch].astype(BF16), cs)
        st = jnp.concatenate([ab[:, :ch], ab[:, ch:]], axis=0).astype(BF16)
        o_ref[:, g * ch:(g + 1) * ch] = _dot(f, st)


def _fourier_context(fz, *, batch, seq, ctx):
    cw = fz.shape[1]
    ch = FOURIER_CH
    cn, sn = _dft_mats(ctx)
    f = (np.concatenate([cn, -sn], axis=1) * ((ctx * ch) ** -0.5)).astype(np.float32)
    cc, sc = _dft_mats(ch)
    cs = np.concatenate([cc, sc], axis=1).astype(np.float32)
    blk0 = batch * seq // ctx
    return pl.pallas_call(
        _fft_ctx_kernel,
        grid=(batch,),
        in_specs=[
            pl.BlockSpec((ctx, cw), lambda b: (blk0 + b, 0)),
            _const_spec(f.shape),
            _const_spec(cs.shape),
        ],
        out_specs=pl.BlockSpec((ctx, cw), lambda b: (b, 0)),
        out_shape=jax.ShapeDtypeStruct((batch * ctx, cw), F32),
        compiler_params=_params("parallel"),
        name="fnet_context",
    )(fz, f, cs)


def _attn_update(s, v, carry):
    m, acc = carry
    m_new = jnp.maximum(m, jnp.max(s, axis=-1, keepdims=True))
    p = jnp.exp2(s - m_new).astype(BF16)
    acc = jnp.exp2(m - m_new) * acc + _dot(p, v)
    return m_new, acc


def _attn_kernel(q_ref, k_ref, v_ref, o_ref, *, kc):
    tq = q_ref.shape[0]
    sl = [slice(0, MLA_SLOT), slice(MLA_SLOT, 2 * MLA_SLOT)]
    qs = [q_ref[:, s] for s in sl]
    init = (jnp.full((tq, 1), -jnp.inf, F32), jnp.zeros((tq, MLA_SLOT), F32))
    carry = (init, init)
    bounds = [(c0, c0 + kc) for c0 in range(0, k_ref.shape[0], kc)]

    def scores(lo, hi):
        return [lax.dot_general(qs[h], k_ref[lo:hi, sl[h]], NT, preferred_element_type=F32) for h in range(2)]

    s_next = scores(*bounds[0])
    for i, (lo, hi) in enumerate(bounds):
        s_cur = s_next
        if i + 1 < len(bounds):
            s_next = scores(*bounds[i + 1])
        carry = tuple(_attn_update(s_cur[h], v_ref[lo:hi, sl[h]], carry[h]) for h in range(2))
    outs = [acc * (1.0 / acc[:, MLA_V:MLA_V + 1]) for (_, acc) in carry]
    lane = lax.broadcasted_iota(jnp.int32, outs[0].shape, 1)
    o_ref[...] = jnp.where(lane < MLA_V, outs[0], pltpu.roll(outs[1], MLA_V, 1)).astype(o_ref.dtype)


def _attention(q, k, v, *, latent, batch, seq, ctx, tq, kc):
    hp = MLA_HEADS // 2
    w = 2 * MLA_SLOT
    wv = 2 * MLA_V
    if latent:
        nq = seq // tq
        q_spec = pl.BlockSpec((tq, w), lambda b, h, i: (b * nq + i, h))
        kv_spec = pl.BlockSpec((seq + ctx, w), lambda b, h, i: (b, h))
    else:
        nq = ctx // tq
        q0 = batch * seq // tq
        per_batch = (seq + ctx) // ctx
        q_spec = pl.BlockSpec((tq, w), lambda b, h, i: (q0 + b * nq + i, h))
        kv_spec = pl.BlockSpec((ctx, w), lambda b, h, i: (b * per_batch + seq // ctx, h))
    return pl.pallas_call(
        functools.partial(_attn_kernel, kc=kc),
        grid=(batch, hp, nq),
        in_specs=[q_spec, kv_spec, kv_spec],
        out_specs=pl.BlockSpec((tq, wv), lambda b, h, i: (b * nq + i, h)),
        out_shape=jax.ShapeDtypeStruct((batch * nq * tq, hp * wv), BF16),
        compiler_params=_params("parallel", "parallel", "arbitrary"),
        name="mla_latent" if latent else "mla_context",
    )(q, k, v)


def _merge_kernel(*refs, n_lat_tiles, with_ctx, n_sub):
    x_ref, mod_ref, gpre_ref, gpost_ref, wgt_ref = refs[:5]
    ya_refs = refs[5:5 + n_sub]
    rest = refs[5 + n_sub:]
    if with_ctx:
        yb_ref, yc_ref, ybc_ref, ycc_ref, wb_ref, wo_ref, o_ref = rest
        is_ctx = pl.program_id(0) >= n_lat_tiles
    else:
        yb_ref, yc_ref, wb_ref, wo_ref, o_ref = rest
    d = x_ref.shape[1]
    sub = x_ref.shape[0] // n_sub
    xs, branches = [], []
    for j in range(n_sub):
        rows = slice(j * sub, (j + 1) * sub)
        yb, yc = yb_ref[rows, :], yc_ref[rows, :]
        if with_ctx:
            yb = jnp.where(is_ctx, ybc_ref[rows, :], yb)
            yc = jnp.where(is_ctx, ycc_ref[rows, :], yc)
        ys = (ya_refs[j][...], yb.astype(BF16), yc)
        branches.append([_dot(ys[br], wb_ref[br]) for br in range(3)])
        xs.append(x_ref[rows, :])
    hs = [_modulated(x, gpre_ref[...], mod_ref, 1).astype(BF16) for x in xs]
    gates = [[_dot(h, wgt_ref[:, br * d:(br + 1) * d]) for br in range(3)] for h in hs]
    ms = []
    for j in range(n_sub):
        m = jnp.zeros(xs[j].shape, F32)
        for br in range(3):
            m = m + jax.nn.sigmoid(gates[j][br]) * branches[j][br]
        ms.append(m.astype(BF16))
    outs = [_dot(m, wo_ref[...]) for m in ms]
    for j in range(n_sub):
        o_ref[j * sub:(j + 1) * sub, :] = xs[j] + mod_ref[5:6, :] * _rms(outs[j], gpost_ref[...])


def _merge(x_all, mods, layer, g_pre, g_post, w_gates, ya, yb, yc, yb_ctx, yc_ctx, wb, wo,
           *, n_rows, n_lat, tm, n_sub, cond_of_tile, kv_of_tile):
    d = x_all.shape[1]
    bw = ya.shape[1]
    with_ctx = yb_ctx is not None
    big = tm * n_sub
    n_lat_tiles = n_lat // big
    row = lambda w: pl.BlockSpec((big, w), lambda i: (i, 0))
    lat = lambda w: pl.BlockSpec((big, w), lambda i: (jnp.minimum(i, n_lat_tiles - 1), 0))
    ctx = lambda w: pl.BlockSpec((big, w), lambda i: (jnp.maximum(i - n_lat_tiles, 0), 0))
    ya_specs = [pl.BlockSpec((tm, bw), lambda i, j=j: (kv_of_tile(i * n_sub + j), 0)) for j in range(n_sub)]
    branch_specs = ya_specs + [lat(bw), lat(bw)] + ([ctx(bw), ctx(bw)] if with_ctx else [])
    branch_args = [ya] * n_sub + [yb, yc] + ([yb_ctx, yc_ctx] if with_ctx else [])
    return pl.pallas_call(
        functools.partial(_merge_kernel, n_lat_tiles=n_lat_tiles, with_ctx=with_ctx, n_sub=n_sub),
        grid=(n_rows // big,),
        in_specs=[
            row(d),
            pl.BlockSpec((None, None, N_MOD, d), lambda i: (layer, cond_of_tile(i * n_sub), 0, 0)),
            _const_spec((1, d)),
            _const_spec((1, d)),
            _const_spec(w_gates.shape),
            *branch_specs,
            pl.BlockSpec((None,) + wb.shape[1:], lambda i: (layer, 0, 0, 0), pipeline_mode=pl.Buffered(1)),
            pl.BlockSpec((None,) + wo.shape[1:], lambda i: (layer, 0, 0), pipeline_mode=pl.Buffered(1)),
        ],
        out_specs=row(d),
        out_shape=jax.ShapeDtypeStruct((n_rows, d), F32),
        compiler_params=_params("parallel"),
        name="merge",
    )(x_all, mods, g_pre.reshape(1, d), g_post.reshape(1, d), w_gates, *branch_args, wb, wo)


def _rope_table(seq, tm):
    pos = np.arange(seq)
    half = MLA_ROPE // 2
    inv_freq = ROPE_BASE ** (-np.arange(0, half, 2) / half)
    ar, ac = (pos // GRID_W)[:, None] * inv_freq, (pos % GRID_W)[:, None] * inv_freq
    cos = np.concatenate([np.cos(ar), np.cos(ar), np.cos(ac), np.cos(ac)], axis=1)
    sin = np.concatenate([-np.sin(ar), np.sin(ar), -np.sin(ac), np.sin(ac)], axis=1)
    pad = LANES - MLA_NOPE - MLA_ROPE
    rows = lambda n, c, s: np.concatenate(
        [np.ones((n, MLA_NOPE)), c, np.zeros((n, pad)), np.zeros((n, MLA_NOPE)), s, np.zeros((n, pad))], axis=1)
    tab = rows(seq, cos, sin)
    ident = rows(tm, np.ones((tm, MLA_ROPE)), np.zeros((tm, MLA_ROPE)))
    return np.concatenate([tab, ident], axis=0).astype(np.float32)


def _with_rope_partners(w):
    j = np.arange(MLA_ROPE)
    h = MLA_ROPE // 4
    partner = np.where((j % (2 * h)) < h, j + h, j - h)
    rope = w[..., -MLA_ROPE:]
    return jnp.concatenate([w, rope[..., partner]], axis=-1)


def _pad_cols(w, n):
    return jnp.pad(w, ((0, 0), (0, n - w.shape[1])))


def _layer_weights(layer, w_in, gla_w_decay, gla_b_decay, mla_w_uq, mla_w_ukv):
    nqk = GLA_HEADS * GLA_DK
    r = GLA_GATE_RANK
    o_dec = 2 * nqk + 2 * GLA_HEADS * GLA_DV
    o_four = o_dec + 2 * r
    o_cq = o_four + FOURIER_GROUPS * FOURIER_CH
    q_rank = mla_w_uq.shape[1]
    kv_rank = mla_w_ukv.shape[1]
    o_ckv = o_cq + q_rank
    o_kr = o_ckv + kv_rank
    o_gates = o_kr + MLA_ROPE
    w = w_in[layer]
    w_in2 = jnp.concatenate([
        w[:, :o_dec], _pad_cols(w[:, o_dec:o_four], LANES), w[:, o_four:o_cq], w[:, o_cq:o_ckv],
        w[:, o_ckv:o_kr], _pad_cols(_with_rope_partners(w[:, o_kr:o_gates]), LANES)], axis=1).astype(BF16)
    w_gates = w[:, o_gates:].astype(BF16)
    wd = gla_w_decay[layer]
    wdec = jnp.zeros((LANES, 2 * nqk), F32)
    wdec = wdec.at[0:r, 0:nqk].set(wd[0]).at[r:2 * r, nqk:].set(wd[1]).astype(BF16)
    bdec = gla_b_decay[layer].reshape(1, 2 * nqk)
    dq = MLA_NOPE + MLA_ROPE
    wuq = _with_rope_partners(mla_w_uq[layer].reshape(q_rank, MLA_HEADS, dq))
    wuq = wuq.reshape(q_rank, MLA_HEADS * MLA_SLOT).astype(BF16)
    wkv = mla_w_ukv[layer].reshape(kv_rank, MLA_HEADS, MLA_NOPE + MLA_V)
    wk = jnp.pad(wkv[:, :, :MLA_NOPE], ((0, 0), (0, 0), (0, MLA_SLOT - MLA_NOPE)))
    wv = jnp.pad(wkv[:, :, MLA_NOPE:], ((0, 0), (0, 0), (0, MLA_SLOT - MLA_V)))
    wukv = jnp.concatenate([wk.reshape(kv_rank, MLA_HEADS * MLA_SLOT),
                            wv.reshape(kv_rank, MLA_HEADS * MLA_SLOT)], axis=1).astype(BF16)
    return w_in2, w_gates, wdec, bdec, wuq, wukv


def kernel(x, c, ctx, c_ctx, w_mod, b_mod, norm_pre, norm_post, ffn_w_gate, ffn_w_up, ffn_w_down, w_in,
           gla_w_decay, gla_b_decay, gla_norm, mla_q_norm, mla_w_uq, mla_kv_norm, mla_w_ukv, w_branch, w_out):
    batch, seq, d = x.shape
    n_ctx = ctx.shape[1]
    depth = w_mod.shape[0]
    n_lat = batch * seq
    t = n_lat + batch * n_ctx
    tm = min(256, n_ctx)
    tm_ffn = FFN_TM if seq % FFN_TM == 0 and (batch * n_ctx) % FFN_TM == 0 else tm
    d_ff = ffn_w_gate.shape[-1]
    fc = 256 if d_ff % 256 == 0 else d_ff
    assert seq % tm == 0 and n_ctx % tm == 0 and seq % (FFT_NA * 8) == 0 and batch < MOD_ROWS

    cond_of_tile = lambda i: jnp.minimum((i * tm) // seq, batch)
    tab_of_tile = lambda i: jnp.where(i * tm < n_lat, ((i * tm) % seq) // tm, seq // tm)
    lat_tiles, ctx_tiles = seq // tm, n_ctx // tm

    def kv_of_tile(i):
        j = i - batch * lat_tiles
        lat = (i // lat_tiles) * (lat_tiles + ctx_tiles) + i % lat_tiles
        ctx_row = (j // ctx_tiles) * (lat_tiles + ctx_tiles) + lat_tiles + j % ctx_tiles
        return jnp.where(j < 0, lat, ctx_row)

    n_keys = seq + n_ctx
    key_chunk = max(kc for kc in range(LANES, min(ATTN_KC, n_keys) + 1, LANES) if n_keys % kc == 0)

    cond = jnp.concatenate([c, c_ctx[None, :], jnp.zeros((MOD_ROWS - batch - 1, d), F32)], axis=0)
    mods = _modulation(cond, w_mod, b_mod, tn=(N_MOD * d) // 8).reshape(depth, MOD_ROWS, N_MOD, d)
    tab = _rope_table(seq, tm)
    x_all, x_ctx = x.reshape(n_lat, d), ctx.reshape(batch * n_ctx, d)
    wg_all, wu_all, wd_all = ffn_w_gate.astype(BF16), ffn_w_up.astype(BF16), ffn_w_down.astype(BF16)
    wb_all, wo_all = w_branch.astype(BF16), w_out.astype(BF16)

    for layer in range(depth):
        last = layer == depth - 1
        ffn = lambda xa, xc, sub, s, rows: _half_ffn(
            xa, xc, mods, layer, sub, s, norm_pre[layer, sub], norm_post[layer, sub], wg_all, wu_all, wd_all,
            n_rows=rows, tm=tm_ffn, cond_of_tile=lambda i: jnp.minimum((i * tm_ffn) // seq, batch), fc=fc)
        x_all = ffn(x_all, x_ctx if layer == 0 else None, 0, 0, t)

        w_in2, w_gates, wdec, bdec, wuq, wukv = _layer_weights(
            layer, w_in, gla_w_decay, gla_b_decay, mla_w_uq, mla_w_ukv)
        gla_in, fz, q, k, v = _in_projection(
            x_all, mods, layer, norm_pre[layer, 1], w_in2, wdec, bdec,
            mla_q_norm[layer].reshape(1, -1), wuq, mla_kv_norm[layer].reshape(1, -1), wukv, tab,
            tm=tm, cond_of_tile=cond_of_tile, tab_of_tile=tab_of_tile, kv_of_tile=kv_of_tile)

        scan = functools.partial(_gla_scan, tb=tm, seq=seq, ctx=n_ctx)
        gla_in = gla_in.reshape(batch, n_keys, gla_in.shape[1])
        o_fwd = scan(gla_in, None, None, reverse=False)
        ya = scan(gla_in, o_fwd, gla_norm[layer], reverse=True).reshape(batch * n_keys, -1)

        rows_out = n_lat if last else t
        yb = _fourier_latent(fz, batch=batch, seq=seq)
        yc = _attention(q, k, v, latent=True, batch=batch, seq=seq, ctx=n_ctx,
                        tq=min(ATTN_TQ, seq), kc=key_chunk)
        yb_ctx = yc_ctx = None
        if not last:
            yb_ctx = _fourier_context(fz, batch=batch, seq=seq, ctx=n_ctx)
            yc_ctx = _attention(q, k, v, latent=False, batch=batch, seq=seq, ctx=n_ctx, tq=tm, kc=n_ctx)

        x_all = _merge(x_all, mods, layer, norm_pre[layer, 1], norm_post[layer, 1], w_gates, ya, yb, yc,
                       yb_ctx, yc_ctx, wb_all, wo_all,
                       n_rows=rows_out, n_lat=n_lat, tm=tm, n_sub=tm_ffn // tm, cond_of_tile=cond_of_tile,
                       kv_of_tile=kv_of_tile)
        x_all = ffn(x_all, None, 2, 1, rows_out)
    return x_all.reshape(batch, seq, d)
```

```python
import functools

import jax
import jax.numpy as jnp
import numpy as np
from jax import lax
from jax.experimental import pallas as pl
from jax.experimental.pallas import tpu as pltpu

F32 = jnp.float32
BF16 = jnp.bfloat16

GRID_W = 64
N_MOD = 9
GLA_HEADS = 4
GLA_DK = 64
GLA_DV = 128
GLA_GATE_RANK = 16
GLA_TAU = 16.0
GLA_CHUNK = 64
GLA_EXP_CLAMP = 80.0
FOURIER_GROUPS = 4
FOURIER_CH = 128
FFT_NA = 64
MLA_HEADS = 8
MLA_NOPE = 64
MLA_ROPE = 32
MLA_V = 64
MLA_SLOT = 128
ROPE_BASE = 10000.0
EPS = 1e-6
LOG2_E = 1.4426950408889634
FFN_TM = 512
ATTN_TQ = 512
ATTN_KC = 4096
LANES = 128
MOD_ROWS = 8
VMEM_LIMIT = 56 * 1024 * 1024

NT = (((1,), (1,)), ((), ()))
TN = (((0,), (0,)), ((), ()))


def _params(*sem):
    return pltpu.CompilerParams(dimension_semantics=sem, vmem_limit_bytes=VMEM_LIMIT)


def _const_spec(shape):
    nd = len(shape)
    return pl.BlockSpec(shape, lambda *_: (0,) * nd, pipeline_mode=pl.Buffered(1))


def _dot(a, b):
    return jnp.dot(a, b, preferred_element_type=F32)


def _rms(x, g):
    return x * lax.rsqrt(jnp.mean(x * x, axis=-1, keepdims=True) + EPS) * g


def _silu(x):
    return x * jax.nn.sigmoid(x)


def _modulated(x, g, mod_ref, sub):
    shift = mod_ref[3 * sub + 0:3 * sub + 1, :]
    scale = mod_ref[3 * sub + 1:3 * sub + 2, :]
    return _rms(x, g) * (1.0 + scale) + shift


def _mod_kernel(c_ref, w_ref, b_ref, o_ref):
    o_ref[0] = _dot(_silu(c_ref[...]), w_ref[0]) + b_ref[0]


def _modulation(cond, w_mod, b_mod, tn):
    depth, d, nm = w_mod.shape
    return pl.pallas_call(
        _mod_kernel,
        grid=(depth, nm // tn),
        in_specs=[
            pl.BlockSpec((MOD_ROWS, d), lambda l, j: (0, 0)),
            pl.BlockSpec((1, d, tn), lambda l, j: (l, 0, j)),
            pl.BlockSpec((1, 1, tn), lambda l, j: (l, 0, j)),
        ],
        out_specs=pl.BlockSpec((1, MOD_ROWS, tn), lambda l, j: (l, 0, j)),
        out_shape=jax.ShapeDtypeStruct((depth, MOD_ROWS, nm), F32),
        compiler_params=_params("parallel", "parallel"),
        name="modulation",
    )(cond, w_mod, b_mod.reshape(depth, 1, nm))


def _ffn_kernel(*refs, sub, fc, n_lat_tiles):
    if n_lat_tiles is None:
        x_ref, mod_ref, gpre_ref, gpost_ref, wg_ref, wu_ref, wd_ref, o_ref = refs
        xc_ref = None
    else:
        x_ref, xc_ref, mod_ref, gpre_ref, gpost_ref, wg_ref, wu_ref, wd_ref, o_ref = refs
    d_ff = wg_ref.shape[1]
    gate = mod_ref[3 * sub + 2:3 * sub + 3, :]
    tm = x_ref.shape[0]
    half = tm // 2 if tm >= 512 else tm
    xs, hs = [], []
    for r0 in range(0, tm, half):
        rows = slice(r0, r0 + half)
        x = x_ref[rows, :]
        if xc_ref is not None:
            x = jnp.where(pl.program_id(0) >= n_lat_tiles, xc_ref[rows, :], x)
        xs.append(x)
        hs.append(_modulated(x, gpre_ref[...], mod_ref, sub).astype(BF16))
    ys = [jnp.zeros(x.shape, F32) for x in xs]
    chunks = list(range(0, d_ff, fc))

    def up(c0):
        wg, wu = wg_ref[:, c0:c0 + fc].astype(BF16), wu_ref[:, c0:c0 + fc].astype(BF16)
        return [_dot(h, wg) for h in hs], [_dot(h, wu) for h in hs]

    nxt = up(chunks[0])
    for i, c0 in enumerate(chunks):
        gs, us = nxt
        if i + 1 < len(chunks):
            nxt = up(chunks[i + 1])
        acts = [(_silu(g) * u).astype(BF16) for g, u in zip(gs, us)]
        wd = wd_ref[c0:c0 + fc, :].astype(BF16)
        ys = [y + _dot(a, wd) for y, a in zip(ys, acts)]
    for i, (x, y) in enumerate(zip(xs, ys)):
        o_ref[i * half:(i + 1) * half, :] = x + 0.5 * gate * _rms(y, gpost_ref[...])


def _half_ffn(x_all, x_ctx, mods, layer, sub, ffn_idx, g_pre, g_post, wg, wu, wd,
              *, n_rows, tm, cond_of_tile, fc):
    d = x_all.shape[1]
    d_ff = wg.shape[-1]
    stacked = lambda r, c: pl.BlockSpec((None, None, r, c), lambda i: (layer, ffn_idx, 0, 0),
                                        pipeline_mode=pl.Buffered(1))
    if x_ctx is None:
        n_lat_tiles = None
        x_specs, xs = [pl.BlockSpec((tm, d), lambda i: (i, 0))], [x_all]
    else:
        n_lat_tiles = x_all.shape[0] // tm
        x_specs = [pl.BlockSpec((tm, d), lambda i: (jnp.minimum(i, n_lat_tiles - 1), 0)),
                   pl.BlockSpec((tm, d), lambda i: (jnp.maximum(i - n_lat_tiles, 0), 0))]
        xs = [x_all, x_ctx]
    return pl.pallas_call(
        functools.partial(_ffn_kernel, sub=sub, fc=fc, n_lat_tiles=n_lat_tiles),
        grid=(n_rows // tm,),
        in_specs=[
            *x_specs,
            pl.BlockSpec((None, None, N_MOD, d), lambda i: (layer, cond_of_tile(i), 0, 0)),
            _const_spec((1, d)),
            _const_spec((1, d)),
            stacked(d, d_ff),
            stacked(d, d_ff),
            stacked(d_ff, d),
        ],
        out_specs=pl.BlockSpec((tm, d), lambda i: (i, 0)),
        out_shape=jax.ShapeDtypeStruct((n_rows, d), F32),
        compiler_params=_params("parallel"),
        name=f"half_ffn_{sub}",
    )(*xs, mods, g_pre.reshape(1, d), g_post.reshape(1, d), wg, wu, wd)


_Z_GLA = 0
_Z_DEC = 1536
_Z_FOUR = 1664
_Z_CQ = 2176
_Z_CKV = 2560
_Z_KR = 2816
_Z_END = 2944


def _rope_slot(x, tab):
    return x * tab[:, :LANES] + pltpu.roll(x, LANES - MLA_ROPE, 1) * tab[:, LANES:]


def _inproj_kernel(x_ref, mod_ref, gpre_ref, win_ref, wdec_ref, bdec_ref, qn_ref, wuq_ref,
                   kvn_ref, wukv_ref, tab_ref, gla_ref, fz_ref, q_ref, k_ref, v_ref):
    h = _modulated(x_ref[...], gpre_ref[...], mod_ref, 1).astype(BF16)
    zm = _dot(h, win_ref[:, _Z_CQ:_Z_END])
    z = _dot(h, win_ref[:, 0:_Z_CQ])
    cq = _rms(zm[:, 0:_Z_CKV - _Z_CQ], qn_ref[...]).astype(BF16)
    ckv = _rms(zm[:, _Z_CKV - _Z_CQ:_Z_KR - _Z_CQ], kvn_ref[...]).astype(BF16)
    q = _dot(cq, wuq_ref[...])
    kv = _dot(ckv, wukv_ref[...])
    nqk = GLA_HEADS * GLA_DK
    gla_ref[:, 0:nqk] = z[:, 0:nqk] * (GLA_DK ** -0.5)
    gla_ref[:, nqk:_Z_DEC] = z[:, nqk:_Z_DEC]
    xd = _dot(z[:, _Z_DEC:_Z_FOUR].astype(BF16), wdec_ref[...]) + bdec_ref[...]
    logsig = jnp.minimum(xd, 0.0) - jnp.log1p(jnp.exp(-jnp.abs(xd)))
    gla_ref[:, _Z_DEC:_Z_DEC + 2 * nqk] = logsig * (1.0 / GLA_TAU)
    fz_ref[...] = z[:, _Z_FOUR:_Z_CQ]
    tab = tab_ref[...]
    nk = MLA_HEADS * MLA_SLOT
    kr = _rope_slot(pltpu.roll(zm[:, _Z_KR - _Z_CQ:], MLA_NOPE, 1), tab)
    scale = (MLA_NOPE + MLA_ROPE) ** -0.5 * LOG2_E
    for hd in range(MLA_HEADS):
        sl = slice(hd * MLA_SLOT, (hd + 1) * MLA_SLOT)
        q_ref[:, sl] = (_rope_slot(q[:, sl], tab) * scale).astype(BF16)
        k_ref[:, sl] = (kv[:, sl] + kr).astype(BF16)
    lane = lax.broadcasted_iota(jnp.int32, (x_ref.shape[0], nk), 1)
    v_ref[...] = jnp.where(lane % MLA_SLOT == MLA_V, 1.0, kv[:, nk:]).astype(BF16)


def _in_projection(x_all, mods, layer, g_pre, w_in2, wdec, bdec, qn, wuq, kvn, wukv, tab,
                   *, tm, cond_of_tile, tab_of_tile, kv_of_tile):
    t, d = x_all.shape
    nq = MLA_HEADS * MLA_SLOT
    row = lambda w: pl.BlockSpec((tm, w), lambda i: (i, 0))
    kv_row = pl.BlockSpec((tm, nq), lambda i: (kv_of_tile(i), 0))
    gla_row = pl.BlockSpec((tm, 2048), lambda i: (kv_of_tile(i), 0))
    return pl.pallas_call(
        _inproj_kernel,
        grid=(t // tm,),
        in_specs=[
            row(d),
            pl.BlockSpec((None, None, N_MOD, d), lambda i: (layer, cond_of_tile(i), 0, 0)),
            _const_spec((1, d)),
            _const_spec(w_in2.shape),
            _const_spec(wdec.shape),
            _const_spec(bdec.shape),
            _const_spec(qn.shape),
            _const_spec(wuq.shape),
            _const_spec(kvn.shape),
            _const_spec(wukv.shape),
            pl.BlockSpec((tm, 2 * LANES), lambda i: (tab_of_tile(i), 0)),
        ],
        out_specs=[gla_row, row(512), row(nq), kv_row, kv_row],
        out_shape=[
            jax.ShapeDtypeStruct((t, 2048), F32),
            jax.ShapeDtypeStruct((t, 512), F32),
            jax.ShapeDtypeStruct((t, nq), BF16),
            jax.ShapeDtypeStruct((t, nq), BF16),
            jax.ShapeDtypeStruct((t, nq), BF16),
        ],
        compiler_params=_params("parallel"),
        name="in_projection",
    )(x_all, mods, g_pre.reshape(1, d), w_in2, wdec, bdec, qn, wuq, kvn, wukv, tab)


def _gla_kernel(*refs, reverse, final):
    if final:
        g_ref, of_ref, gn_ref, o_ref, st_ref = refs
    else:
        g_ref, o_ref, st_ref = refs
    c = GLA_CHUNK
    n_batch, tb = g_ref.shape[0], g_ref.shape[1]
    nqk = GLA_HEADS * GLA_DK
    nv = GLA_HEADS * GLA_DV

    @pl.when(pl.program_id(0) == 0)
    def _():
        st_ref[...] = jnp.zeros(st_ref.shape, F32)

    row = lax.broadcasted_iota(jnp.int32, (c, c), 0)
    col = lax.broadcasted_iota(jnp.int32, (c, c), 1)
    keep = (col >= row) if reverse else (col <= row)
    lmat = jnp.where(keep, 1.0, 0.0).astype(BF16)
    la_off = _Z_DEC + (nqk if reverse else 0)
    mid = c // 2 if reverse else c // 2 - 1
    end = 0 if reverse else c - 1
    chunks = range(tb // c)
    order = list(reversed(chunks) if reverse else chunks)
    work = [(bi, slice(ci * c, (ci + 1) * c)) for bi in range(n_batch) for ci in order]
    heads = [(slice(hd * GLA_DK, (hd + 1) * GLA_DK), slice(hd * GLA_DV, (hd + 1) * GLA_DV))
             for hd in range(GLA_HEADS)]

    cums = []
    for bi, rows in work:
        la = g_ref[bi, rows, la_off:la_off + nqk]
        h1 = la.astype(BF16)
        r1 = la - h1.astype(F32)
        h2 = r1.astype(BF16)
        h3 = (r1 - h2.astype(F32)).astype(BF16)
        cums.append(_dot(lmat, h1) + _dot(lmat, h2) + _dot(lmat, h3))
    scaled = []
    for (bi, rows), b in zip(work, cums):
        ref = b[mid:mid + 1, :]
        tot = b[end:end + 1, :]
        qt = g_ref[bi, rows, 0:nqk] * jnp.exp(b - ref)
        kt = g_ref[bi, rows, nqk:2 * nqk] * jnp.exp(jnp.minimum(ref - b, GLA_EXP_CLAMP))
        q_in = (qt * jnp.exp(ref)).astype(BF16)
        k_st = (kt * jnp.exp(tot - ref)).astype(BF16)
        scaled.append((qt.astype(BF16), kt.astype(BF16), q_in, k_st, jnp.exp(tot)))
    scores = []
    for qt, kt, _, _, _ in scaled:
        scores.append([jnp.where(keep, lax.dot_general(qt[:, ks], kt[:, ks], NT, preferred_element_type=F32), 0.0)
                       .astype(BF16) for ks, _ in heads])
    intra, upd = [], []
    for (bi, rows), a, (_, _, _, k_st, _) in zip(work, scores, scaled):
        vals = [g_ref[bi, rows, 2 * nqk + vs.start:2 * nqk + vs.stop].astype(BF16) for _, vs in heads]
        intra.append([_dot(a[hd], vals[hd]) for hd in range(GLA_HEADS)])
        upd.append([lax.dot_general(vals[hd], k_st[:, heads[hd][0]], TN, preferred_element_type=F32)
                    for hd in range(GLA_HEADS)])
    states = {bi: [st_ref[bi, hd] for hd in range(GLA_HEADS)] for bi in range(n_batch)}
    for (bi, rows), o_in, u, (_, _, q_in, _, dec) in zip(work, intra, upd, scaled):
        for hd, (ks, vs) in enumerate(heads):
            st = states[bi][hd]
            o = o_in[hd] + lax.dot_general(q_in[:, ks], st.astype(BF16), NT, preferred_element_type=F32)
            states[bi][hd] = st * dec[:, ks] + u[hd]
            if final:
                o = o + of_ref[bi, rows, vs]
                gate = g_ref[bi, rows, 2 * nqk + nv + vs.start:2 * nqk + nv + vs.stop]
                o_ref[bi, rows, vs] = (_rms(o, gn_ref[...]) * _silu(gate)).astype(o_ref.dtype)
            else:
                o_ref[bi, rows, vs] = o
    for bi in range(n_batch):
        for hd in range(GLA_HEADS):
            st_ref[bi, hd] = states[bi][hd]


def _gla_scan(gla_in, o_fwd, g_norm, *, reverse, tb, seq, ctx):
    batch = gla_in.shape[0]
    n_lat, n_ctx = seq // tb, ctx // tb

    def blk(j):
        if reverse:
            return jnp.where(j < n_ctx, n_lat + (n_ctx - 1 - j), n_lat + n_ctx - 1 - j)
        return jnp.where(j < n_ctx, n_lat + j, j - n_ctx)

    final = o_fwd is not None
    dv = GLA_HEADS * GLA_DV
    in_specs = [pl.BlockSpec((batch, tb, gla_in.shape[2]), lambda j: (0, blk(j), 0))]
    args = [gla_in]
    if final:
        in_specs += [pl.BlockSpec((batch, tb, dv), lambda j: (0, blk(j), 0)), _const_spec((1, GLA_DV))]
        args += [o_fwd, g_norm.reshape(1, GLA_DV)]
    return pl.pallas_call(
        functools.partial(_gla_kernel, reverse=reverse, final=final),
        grid=(n_lat + n_ctx,),
        in_specs=in_specs,
        out_specs=pl.BlockSpec((batch, tb, dv), lambda j: (0, blk(j), 0)),
        out_shape=jax.ShapeDtypeStruct((batch, seq + ctx, dv), BF16 if final else F32),
        scratch_shapes=[pltpu.VMEM((batch, GLA_HEADS, GLA_DV, GLA_DK), F32)],
        compiler_params=_params("arbitrary"),
        name="gla_bwd_scan" if reverse else "gla_fwd_scan",
    )(*args)


def _dft_mats(n):
    k = np.arange(n)
    ang = (2.0 * np.pi / n) * ((k[:, None] * k[None, :]) % n)
    return np.cos(ang), np.sin(ang)


def _fft_kernel(x_ref, fa_ref, twc_ref, tws_ref, mb_ref, cs_ref, o_ref, xt_ref, zr_ref, zi_ref,
                *, na, nb, bg, kg, pitch):
    ch = x_ref.shape[1]
    fa = fa_ref[...].astype(BF16)
    for a in range(na):
        xt_ref[pl.ds(a, nb, stride=pitch), :] = x_ref[a * nb:(a + 1) * nb, :]
    for g0 in range(0, nb, bg * kg):
        starts = range(g0, min(g0 + bg * kg, nb), bg)
        xs = [jnp.concatenate([xt_ref[(b0 + i) * pitch:(b0 + i) * pitch + na, :] for i in range(bg)],
                              axis=1).astype(BF16) for b0 in starts]
        zs = [_dot(fa, x) for x in xs]
        for b0, z in zip(starts, zs):
            for i in range(bg):
                r0 = (b0 + i) * pitch
                zr_ref[r0:r0 + na, :] = z[:na, i * ch:(i + 1) * ch]
                zi_ref[r0:r0 + na, :] = z[na:, i * ch:(i + 1) * ch]
    mb = mb_ref[...].astype(BF16)
    cs = cs_ref[...].astype(BF16)

    for k0 in range(0, na, kg):
        group = range(k0, min(k0 + kg, na))
        ts = []
        for ka in group:
            zr, zi = zr_ref[pl.ds(ka, nb, stride=pitch), :], zi_ref[pl.ds(ka, nb, stride=pitch), :]
            c = twc_ref[ka * nb:(ka + 1) * nb, :]
            s = tws_ref[ka * nb:(ka + 1) * nb, :]
            ts.append(jnp.concatenate([zr * c + zi * s, zi * c - zr * s], axis=0).astype(BF16))
        vs = [_dot(mb, t) for t in ts]
        vvs = [jnp.concatenate([v[:nb], v[nb:]], axis=1).astype(BF16) for v in vs]
        outs = [_dot(vv, cs) for vv in vvs]
        for ka, out in zip(group, outs):
            o_ref[pl.ds(ka, nb, stride=na), :] = out


def _fourier_latent(fz, *, batch, seq):
    cw = fz.shape[1]
    na, nb = FFT_NA, seq // FFT_NA
    pitch = na + 1
    ch = FOURIER_CH
    ca, sa = _dft_mats(na)
    f_a = np.concatenate([ca, -sa], axis=0).astype(np.float32)
    cb, sb = _dft_mats(nb)
    m_b = np.concatenate([np.concatenate([cb, sb], axis=1),
                          np.concatenate([-sb, cb], axis=1)], axis=0).astype(np.float32)
    cc, sc = _dft_mats(ch)
    cs = (np.concatenate([cc, sc], axis=0) * ((seq * ch) ** -0.5)).astype(np.float32)
    ang = (2.0 * np.pi / seq) * ((np.arange(na)[:, None] * np.arange(nb)[None, :]) % seq)
    twc = jnp.broadcast_to(np.cos(ang).reshape(seq, 1).astype(np.float32), (seq, ch))
    tws = jnp.broadcast_to(np.sin(ang).reshape(seq, 1).astype(np.float32), (seq, ch))
    return pl.pallas_call(
        functools.partial(_fft_kernel, na=na, nb=nb, bg=4, kg=16, pitch=pitch),
        grid=(batch, cw // ch),
        in_specs=[pl.BlockSpec((seq, ch), lambda b, g: (b, g)), _const_spec(f_a.shape),
                  _const_spec((seq, ch)), _const_spec((seq, ch)), _const_spec(m_b.shape), _const_spec(cs.shape)],
        out_specs=pl.BlockSpec((seq, ch), lambda b, g: (b, g)),
        out_shape=jax.ShapeDtypeStruct((batch * seq, cw), F32),
        scratch_shapes=[pltpu.VMEM((nb * pitch, ch), F32)] * 3,
        compiler_params=_params("parallel", "parallel"),
        name="fnet_latent",
    )(fz, f_a, twc, tws, m_b, cs)


def _fft_ctx_kernel(x_ref, f_ref, cs_ref, o_ref):
    ch = FOURIER_CH
    f = f_ref[...].astype(BF16)
    cs = cs_ref[...].astype(BF16)
    for g in range(x_ref.shape[1] // ch):
        ab = _dot(x_ref[:, g * ch:(g + 1) * ch].astype(BF16), cs)
        st = jnp.concatenate([ab[:, :ch], ab[:, ch:]], axis=0).astype(BF16)
        o_ref[:, g * ch:(g + 1) * ch] = _dot(f, st)


def _fourier_context(fz, *, batch, seq, ctx):
    cw = fz.shape[1]
    ch = FOURIER_CH
    cn, sn = _dft_mats(ctx)
    f = (np.concatenate([cn, -sn], axis=1) * ((ctx * ch) ** -0.5)).astype(np.float32)
    cc, sc = _dft_mats(ch)
    cs = np.concatenate([cc, sc], axis=1).astype(np.float32)
    blk0 = batch * seq // ctx
    return pl.pallas_call(
        _fft_ctx_kernel,
        grid=(batch,),
        in_specs=[
            pl.BlockSpec((ctx, cw), lambda b: (blk0 + b, 0)),
            _const_spec(f.shape),
            _const_spec(cs.shape),
        ],
        out_specs=pl.BlockSpec((ctx, cw), lambda b: (b, 0)),
        out_shape=jax.ShapeDtypeStruct((batch * ctx, cw), F32),
        compiler_params=_params("parallel"),
        name="fnet_context",
    )(fz, f, cs)


def _attn_update(s, v, carry):
    m, acc = carry
    m_new = jnp.maximum(m, jnp.max(s, axis=-1, keepdims=True))
    p = jnp.exp2(s - m_new).astype(BF16)
    acc = jnp.exp2(m - m_new) * acc + _dot(p, v)
    return m_new, acc


def _attn_kernel(q_ref, k_ref, v_ref, o_ref, *, kc):
    tq = q_ref.shape[0]
    sl = [slice(0, MLA_SLOT), slice(MLA_SLOT, 2 * MLA_SLOT)]
    qs = [q_ref[:, s] for s in sl]
    init = (jnp.full((tq, 1), -jnp.inf, F32), jnp.zeros((tq, MLA_SLOT), F32))
    carry = (init, init)
    bounds = [(c0, c0 + kc) for c0 in range(0, k_ref.shape[0], kc)]

    def scores(lo, hi):
        return [lax.dot_general(qs[h], k_ref[lo:hi, sl[h]], NT, preferred_element_type=F32) for h in range(2)]

    s_next = scores(*bounds[0])
    for i, (lo, hi) in enumerate(bounds):
        s_cur = s_next
        if i + 1 < len(bounds):
            s_next = scores(*bounds[i + 1])
        carry = tuple(_attn_update(s_cur[h], v_ref[lo:hi, sl[h]], carry[h]) for h in range(2))
    outs = [acc * (1.0 / acc[:, MLA_V:MLA_V + 1]) for (_, acc) in carry]
    lane = lax.broadcasted_iota(jnp.int32, outs[0].shape, 1)
    o_ref[...] = jnp.where(lane < MLA_V, outs[0], pltpu.roll(outs[1], MLA_V, 1)).astype(o_ref.dtype)


def _attention(q, k, v, *, latent, batch, seq, ctx, tq, kc):
    hp = MLA_HEADS // 2
    w = 2 * MLA_SLOT
    wv = 2 * MLA_V
    if latent:
        nq = seq // tq
        q_spec = pl.BlockSpec((tq, w), lambda b, h, i: (b * nq + i, h))
        kv_spec = pl.BlockSpec((seq + ctx, w), lambda b, h, i: (b, h))
    else:
        nq = ctx // tq
        q0 = batch * seq // tq
        per_batch = (seq + ctx) // ctx
        q_spec = pl.BlockSpec((tq, w), lambda b, h, i: (q0 + b * nq + i, h))
        kv_spec = pl.BlockSpec((ctx, w), lambda b, h, i: (b * per_batch + seq // ctx, h))
    return pl.pallas_call(
        functools.partial(_attn_kernel, kc=kc),
        grid=(batch, hp, nq),
        in_specs=[q_spec, kv_spec, kv_spec],
        out_specs=pl.BlockSpec((tq, wv), lambda b, h, i: (b * nq + i, h)),
        out_shape=jax.ShapeDtypeStruct((batch * nq * tq, hp * wv), BF16),
        compiler_params=_params("parallel", "parallel", "arbitrary"),
        name="mla_latent" if latent else "mla_context",
    )(q, k, v)


def _merge_kernel(*refs, n_lat_tiles, with_ctx, n_sub):
    x_ref, mod_ref, gpre_ref, gpost_ref, wgt_ref = refs[:5]
    ya_refs = refs[5:5 + n_sub]
    rest = refs[5 + n_sub:]
    if with_ctx:
        yb_ref, yc_ref, ybc_ref, ycc_ref, wb_ref, wo_ref, o_ref = rest
        is_ctx = pl.program_id(0) >= n_lat_tiles
    else:
        yb_ref, yc_ref, wb_ref, wo_ref, o_ref = rest
    d = x_ref.shape[1]
    sub = x_ref.shape[0] // n_sub
    xs, branches = [], []
    for j in range(n_sub):
        rows = slice(j * sub, (j + 1) * sub)
        yb, yc = yb_ref[rows, :], yc_ref[rows, :]
        if with_ctx:
            yb = jnp.where(is_ctx, ybc_ref[rows, :], yb)
            yc = jnp.where(is_ctx, ycc_ref[rows, :], yc)
        ys = (ya_refs[j][...], yb.astype(BF16), yc)
        branches.append([_dot(ys[br], wb_ref[br]) for br in range(3)])
        xs.append(x_ref[rows, :])
    hs = [_modulated(x, gpre_ref[...], mod_ref, 1).astype(BF16) for x in xs]
    gates = [[_dot(h, wgt_ref[:, br * d:(br + 1) * d]) for br in range(3)] for h in hs]
    ms = []
    for j in range(n_sub):
        m = jnp.zeros(xs[j].shape, F32)
        for br in range(3):
            m = m + jax.nn.sigmoid(gates[j][br]) * branches[j][br]
        ms.append(m.astype(BF16))
    outs = [_dot(m, wo_ref[...]) for m in ms]
    for j in range(n_sub):
        o_ref[j * sub:(j + 1) * sub, :] = xs[j] + mod_ref[5:6, :] * _rms(outs[j], gpost_ref[...])


def _merge(x_all, mods, layer, g_pre, g_post, w_gates, ya, yb, yc, yb_ctx, yc_ctx, wb, wo,
           *, n_rows, n_lat, tm, n_sub, cond_of_tile, kv_of_tile):
    d = x_all.shape[1]
    bw = ya.shape[1]
    with_ctx = yb_ctx is not None
    big = tm * n_sub
    n_lat_tiles = n_lat // big
    row = lambda w: pl.BlockSpec((big, w), lambda i: (i, 0))
    lat = lambda w: pl.BlockSpec((big, w), lambda i: (jnp.minimum(i, n_lat_tiles - 1), 0))
    ctx = lambda w: pl.BlockSpec((big, w), lambda i: (jnp.maximum(i - n_lat_tiles, 0), 0))
    ya_specs = [pl.BlockSpec((tm, bw), lambda i, j=j: (kv_of_tile(i * n_sub + j), 0)) for j in range(n_sub)]
    branch_specs = ya_specs + [lat(bw), lat(bw)] + ([ctx(bw), ctx(bw)] if with_ctx else [])
    branch_args = [ya] * n_sub + [yb, yc] + ([yb_ctx, yc_ctx] if with_ctx else [])
    return pl.pallas_call(
        functools.partial(_merge_kernel, n_lat_tiles=n_lat_tiles, with_ctx=with_ctx, n_sub=n_sub),
        grid=(n_rows // big,),
        in_specs=[
            row(d),
            pl.BlockSpec((None, None, N_MOD, d), lambda i: (layer, cond_of_tile(i * n_sub), 0, 0)),
            _const_spec((1, d)),
            _const_spec((1, d)),
            _const_spec(w_gates.shape),
            *branch_specs,
            pl.BlockSpec((None,) + wb.shape[1:], lambda i: (layer, 0, 0, 0), pipeline_mode=pl.Buffered(1)),
            pl.BlockSpec((None,) + wo.shape[1:], lambda i: (layer, 0, 0), pipeline_mode=pl.Buffered(1)),
        ],
        out_specs=row(d),
        out_shape=jax.ShapeDtypeStruct((n_rows, d), F32),
        compiler_params=_params("parallel"),
        name="merge",
    )(x_all, mods, g_pre.reshape(1, d), g_post.reshape(1, d), w_gates, *branch_args, wb, wo)


def _rope_table(seq, tm):
    pos = np.arange(seq)
    half = MLA_ROPE // 2
    inv_freq = ROPE_BASE ** (-np.arange(0, half, 2) / half)
    ar, ac = (pos // GRID_W)[:, None] * inv_freq, (pos % GRID_W)[:, None] * inv_freq
    cos = np.concatenate([np.cos(ar), np.cos(ar), np.cos(ac), np.cos(ac)], axis=1)
    sin = np.concatenate([-np.sin(ar), np.sin(ar), -np.sin(ac), np.sin(ac)], axis=1)
    pad = LANES - MLA_NOPE - MLA_ROPE
    rows = lambda n, c, s: np.concatenate(
        [np.ones((n, MLA_NOPE)), c, np.zeros((n, pad)), np.zeros((n, MLA_NOPE)), s, np.zeros((n, pad))], axis=1)
    tab = rows(seq, cos, sin)
    ident = rows(tm, np.ones((tm, MLA_ROPE)), np.zeros((tm, MLA_ROPE)))
    return np.concatenate([tab, ident], axis=0).astype(np.float32)


def _with_rope_partners(w):
    j = np.arange(MLA_ROPE)
    h = MLA_ROPE // 4
    partner = np.where((j % (2 * h)) < h, j + h, j - h)
    rope = w[..., -MLA_ROPE:]
    return jnp.concatenate([w, rope[..., partner]], axis=-1)


def _pad_cols(w, n):
    return jnp.pad(w, ((0, 0), (0, n - w.shape[1])))


def _layer_weights(layer, w_in, gla_w_decay, gla_b_decay, mla_w_uq, mla_w_ukv):
    nqk = GLA_HEADS * GLA_DK
    r = GLA_GATE_RANK
    o_dec = 2 * nqk + 2 * GLA_HEADS * GLA_DV
    o_four = o_dec + 2 * r
    o_cq = o_four + FOURIER_GROUPS * FOURIER_CH
    q_rank = mla_w_uq.shape[1]
    kv_rank = mla_w_ukv.shape[1]
    o_ckv = o_cq + q_rank
    o_kr = o_ckv + kv_rank
    o_gates = o_kr + MLA_ROPE
    w = w_in[layer]
    w_in2 = jnp.concatenate([
        w[:, :o_dec], _pad_cols(w[:, o_dec:o_four], LANES), w[:, o_four:o_cq], w[:, o_cq:o_ckv],
        w[:, o_ckv:o_kr], _pad_cols(_with_rope_partners(w[:, o_kr:o_gates]), LANES)], axis=1).astype(BF16)
    w_gates = w[:, o_gates:].astype(BF16)
    wd = gla_w_decay[layer]
    wdec = jnp.zeros((LANES, 2 * nqk), F32)
    wdec = wdec.at[0:r, 0:nqk].set(wd[0]).at[r:2 * r, nqk:].set(wd[1]).astype(BF16)
    bdec = gla_b_decay[layer].reshape(1, 2 * nqk)
    dq = MLA_NOPE + MLA_ROPE
    wuq = _with_rope_partners(mla_w_uq[layer].reshape(q_rank, MLA_HEADS, dq))
    wuq = wuq.reshape(q_rank, MLA_HEADS * MLA_SLOT).astype(BF16)
    wkv = mla_w_ukv[layer].reshape(kv_rank, MLA_HEADS, MLA_NOPE + MLA_V)
    wk = jnp.pad(wkv[:, :, :MLA_NOPE], ((0, 0), (0, 0), (0, MLA_SLOT - MLA_NOPE)))
    wv = jnp.pad(wkv[:, :, MLA_NOPE:], ((0, 0), (0, 0), (0, MLA_SLOT - MLA_V)))
    wukv = jnp.concatenate([wk.reshape(kv_rank, MLA_HEADS * MLA_SLOT),
                            wv.reshape(kv_rank, MLA_HEADS * MLA_SLOT)], axis=1).astype(BF16)
    return w_in2, w_gates, wdec, bdec, wuq, wukv


def kernel(x, c, ctx, c_ctx, w_mod, b_mod, norm_pre, norm_post, ffn_w_gate, ffn_w_up, ffn_w_down, w_in,
           gla_w_decay, gla_b_decay, gla_norm, mla_q_norm, mla_w_uq, mla_kv_norm, mla_w_ukv, w_branch, w_out):
    batch, seq, d = x.shape
    n_ctx = ctx.shape[1]
    depth = w_mod.shape[0]
    n_lat = batch * seq
    t = n_lat + batch * n_ctx
    tm = min(256, n_ctx)
    tm_ffn = FFN_TM if seq % FFN_TM == 0 and (batch * n_ctx) % FFN_TM == 0 else tm
    d_ff = ffn_w_gate.shape[-1]
    fc = 256 if d_ff % 256 == 0 else d_ff
    assert seq % tm == 0 and n_ctx % tm == 0 and seq % (FFT_NA * 8) == 0 and batch < MOD_ROWS

    cond_of_tile = lambda i: jnp.minimum((i * tm) // seq, batch)
    tab_of_tile = lambda i: jnp.where(i * tm < n_lat, ((i * tm) % seq) // tm, seq // tm)
    lat_tiles, ctx_tiles = seq // tm, n_ctx // tm

    def kv_of_tile(i):
        j = i - batch * lat_tiles
        lat = (i // lat_tiles) * (lat_tiles + ctx_tiles) + i % lat_tiles
        ctx_row = (j // ctx_tiles) * (lat_tiles + ctx_tiles) + lat_tiles + j % ctx_tiles
        return jnp.where(j < 0, lat, ctx_row)

    n_keys = seq + n_ctx
    key_chunk = max(kc for kc in range(LANES, min(ATTN_KC, n_keys) + 1, LANES) if n_keys % kc == 0)

    cond = jnp.concatenate([c, c_ctx[None, :], jnp.zeros((MOD_ROWS - batch - 1, d), F32)], axis=0)
    mods = _modulation(cond, w_mod, b_mod, tn=(N_MOD * d) // 8).reshape(depth, MOD_ROWS, N_MOD, d)
    tab = _rope_table(seq, tm)
    x_all, x_ctx = x.reshape(n_lat, d), ctx.reshape(batch * n_ctx, d)
    wg_all, wu_all, wd_all = ffn_w_gate, ffn_w_up, ffn_w_down
    wb_all, wo_all = w_branch.astype(BF16), w_out.astype(BF16)

    for layer in range(depth):
        last = layer == depth - 1
        ffn = lambda xa, xc, sub, s, rows: _half_ffn(
            xa, xc, mods, layer, sub, s, norm_pre[layer, sub], norm_post[layer, sub], wg_all, wu_all, wd_all,
            n_rows=rows, tm=tm_ffn, cond_of_tile=lambda i: jnp.minimum((i * tm_ffn) // seq, batch), fc=fc)
        x_all = ffn(x_all, x_ctx if layer == 0 else None, 0, 0, t)

        w_in2, w_gates, wdec, bdec, wuq, wukv = _layer_weights(
            layer, w_in, gla_w_decay, gla_b_decay, mla_w_uq, mla_w_ukv)
        gla_in, fz, q, k, v = _in_projection(
            x_all, mods, layer, norm_pre[layer, 1], w_in2, wdec, bdec,
            mla_q_norm[layer].reshape(1, -1), wuq, mla_kv_norm[layer].reshape(1, -1), wukv, tab,
            tm=tm, cond_of_tile=cond_of_tile, tab_of_tile=tab_of_tile, kv_of_tile=kv_of_tile)

        scan = functools.partial(_gla_scan, tb=tm, seq=seq, ctx=n_ctx)
        gla_in = gla_in.reshape(batch, n_keys, gla_in.shape[1])
        o_fwd = scan(gla_in, None, None, reverse=False)
        ya = scan(gla_in, o_fwd, gla_norm[layer], reverse=True).reshape(batch * n_keys, -1)

        rows_out = n_lat if last else t
        yb = _fourier_latent(fz, batch=batch, seq=seq)
        yc = _attention(q, k, v, latent=True, batch=batch, seq=seq, ctx=n_ctx,
                        tq=min(ATTN_TQ, seq), kc=key_chunk)
        yb_ctx = yc_ctx = None
        if not last:
            yb_ctx = _fourier_context(fz, batch=batch, seq=seq, ctx=n_ctx)
            yc_ctx = _attention(q, k, v, latent=False, batch=batch, seq=seq, ctx=n_ctx, tq=tm, kc=n_ctx)

        x_all = _merge(x_all, mods, layer, norm_pre[layer, 1], norm_post[layer, 1], w_gates, ya, yb, yc,
                       yb_ctx, yc_ctx, wb_all, wo_all,
                       n_rows=rows_out, n_lat=n_lat, tm=tm, n_sub=tm_ffn // tm, cond_of_tile=cond_of_tile,
                       kv_of_tile=kv_of_tile)
        x_all = ffn(x_all, None, 2, 1, rows_out)
    return x_all.reshape(batch, seq, d)
```

```python
import functools

import jax
import jax.numpy as jnp
import numpy as np
from jax import lax
from jax.experimental import pallas as pl
from jax.experimental.pallas import tpu as pltpu

F32 = jnp.float32
BF16 = jnp.bfloat16

GRID_W = 64
N_MOD = 9
GLA_HEADS = 4
GLA_DK = 64
GLA_DV = 128
GLA_GATE_RANK = 16
GLA_TAU = 16.0
GLA_CHUNK = 64
GLA_EXP_CLAMP = 80.0
FOURIER_GROUPS = 4
FOURIER_CH = 128
FFT_NA = 64
MLA_HEADS = 8
MLA_NOPE = 64
MLA_ROPE = 32
MLA_V = 64
MLA_SLOT = 128
ROPE_BASE = 10000.0
EPS = 1e-6
LOG2_E = 1.4426950408889634
FFN_TM = 512
ATTN_TQ = 512
ATTN_KC = 4096
LANES = 128
MOD_ROWS = 8
VMEM_LIMIT = 56 * 1024 * 1024

NT = (((1,), (1,)), ((), ()))
TN = (((0,), (0,)), ((), ()))


def _params(*sem):
    return pltpu.CompilerParams(dimension_semantics=sem, vmem_limit_bytes=VMEM_LIMIT)


def _const_spec(shape):
    nd = len(shape)
    return pl.BlockSpec(shape, lambda *_: (0,) * nd, pipeline_mode=pl.Buffered(1))


def _dot(a, b):
    return jnp.dot(a, b, preferred_element_type=F32)


def _rms(x, g):
    return x * lax.rsqrt(jnp.mean(x * x, axis=-1, keepdims=True) + EPS) * g


def _silu(x):
    return x * jax.nn.sigmoid(x)


def _modulated(x, g, mod_ref, sub):
    shift = mod_ref[3 * sub + 0:3 * sub + 1, :]
    scale = mod_ref[3 * sub + 1:3 * sub + 2, :]
    return _rms(x, g) * (1.0 + scale) + shift


def _mod_kernel(c_ref, w_ref, b_ref, o_ref):
    part = _dot(_silu(c_ref[...]), w_ref[...])
    first = pl.program_id(1) == 0

    @pl.when(first)
    def _():
        o_ref[...] = part + b_ref[...]

    @pl.when(jnp.logical_not(first))
    def _():
        o_ref[...] += part


def _modulation(cond, w_mod, b_mod, tk):
    depth, d, nm = w_mod.shape
    nk = d // tk
    cond_k = cond.reshape(MOD_ROWS, nk, tk).transpose(1, 0, 2)
    return pl.pallas_call(
        _mod_kernel,
        grid=(depth, nk),
        in_specs=[
            pl.BlockSpec((None, MOD_ROWS, tk), lambda l, k: (k, 0, 0)),
            pl.BlockSpec((None, tk, nm), lambda l, k: (l, k, 0)),
            pl.BlockSpec((None, 1, nm), lambda l, k: (l, 0, 0)),
        ],
        out_specs=pl.BlockSpec((None, MOD_ROWS, nm), lambda l, k: (l, 0, 0)),
        out_shape=jax.ShapeDtypeStruct((depth, MOD_ROWS, nm), F32),
        compiler_params=_params("parallel", "arbitrary"),
        name="modulation",
    )(cond_k, w_mod, b_mod.reshape(depth, 1, nm))


def _ffn_kernel(*refs, sub, fc, n_lat_tiles):
    if n_lat_tiles is None:
        x_ref, mod_ref, gpre_ref, gpost_ref, wg_ref, wu_ref, wd_ref, o_ref = refs
        xc_ref = None
    else:
        x_ref, xc_ref, mod_ref, gpre_ref, gpost_ref, wg_ref, wu_ref, wd_ref, o_ref = refs
    d_ff = wg_ref.shape[1]
    gate = mod_ref[3 * sub + 2:3 * sub + 3, :]
    tm = x_ref.shape[0]
    half = tm // 2 if tm >= 512 else tm
    xs, hs = [], []
    for r0 in range(0, tm, half):
        rows = slice(r0, r0 + half)
        x = x_ref[rows, :]
        if xc_ref is not None:
            x = jnp.where(pl.program_id(0) >= n_lat_tiles, xc_ref[rows, :], x)
        xs.append(x)
        hs.append(_modulated(x, gpre_ref[...], mod_ref, sub).astype(BF16))
    ys = [jnp.zeros(x.shape, F32) for x in xs]
    chunks = list(range(0, d_ff, fc))

    def up(c0):
        wg, wu = wg_ref[:, c0:c0 + fc].astype(BF16), wu_ref[:, c0:c0 + fc].astype(BF16)
        return [_dot(h, wg) for h in hs], [_dot(h, wu) for h in hs]

    nxt = up(chunks[0])
    for i, c0 in enumerate(chunks):
        gs, us = nxt
        if i + 1 < len(chunks):
            nxt = up(chunks[i + 1])
        acts = [(_silu(g) * u).astype(BF16) for g, u in zip(gs, us)]
        wd = wd_ref[c0:c0 + fc, :].astype(BF16)
        ys = [y + _dot(a, wd) for y, a in zip(ys, acts)]
    for i, (x, y) in enumerate(zip(xs, ys)):
        o_ref[i * half:(i + 1) * half, :] = x + 0.5 * gate * _rms(y, gpost_ref[...])


def _half_ffn(x_all, x_ctx, mods, layer, sub, ffn_idx, g_pre, g_post, wg, wu, wd,
              *, n_rows, tm, cond_of_tile, fc):
    d = x_all.shape[1]
    d_ff = wg.shape[-1]
    stacked = lambda r, c: pl.BlockSpec((None, None, r, c), lambda i: (layer, ffn_idx, 0, 0),
                                        pipeline_mode=pl.Buffered(1))
    if x_ctx is None:
        n_lat_tiles = None
        x_specs, xs = [pl.BlockSpec((tm, d), lambda i: (i, 0))], [x_all]
    else:
        n_lat_tiles = x_all.shape[0] // tm
        x_specs = [pl.BlockSpec((tm, d), lambda i: (jnp.minimum(i, n_lat_tiles - 1), 0)),
                   pl.BlockSpec((tm, d), lambda i: (jnp.maximum(i - n_lat_tiles, 0), 0))]
        xs = [x_all, x_ctx]
    return pl.pallas_call(
        functools.partial(_ffn_kernel, sub=sub, fc=fc, n_lat_tiles=n_lat_tiles),
        grid=(n_rows // tm,),
        in_specs=[
            *x_specs,
            pl.BlockSpec((None, None, N_MOD, d), lambda i: (layer, cond_of_tile(i), 0, 0)),
            _const_spec((1, d)),
            _const_spec((1, d)),
            stacked(d, d_ff),
            stacked(d, d_ff),
            stacked(d_ff, d),
        ],
        out_specs=pl.BlockSpec((tm, d), lambda i: (i, 0)),
        out_shape=jax.ShapeDtypeStruct((n_rows, d), F32),
        compiler_params=_params("parallel"),
        name=f"half_ffn_{sub}",
    )(*xs, mods, g_pre.reshape(1, d), g_post.reshape(1, d), wg, wu, wd)


_Z_GLA = 0
_Z_DEC = 1536
_Z_FOUR = 1664
_Z_CQ = 2176
_Z_CKV = 2560
_Z_KR = 2816
_Z_END = 2944


def _rope_slot(x, tab):
    return x * tab[:, :LANES] + pltpu.roll(x, LANES - MLA_ROPE, 1) * tab[:, LANES:]


def _inproj_kernel(x_ref, mod_ref, gpre_ref, win_ref, wdec_ref, bdec_ref, qn_ref, wuq_ref,
                   kvn_ref, wukv_ref, tab_ref, gla_ref, fz_ref, q_ref, k_ref, v_ref):
    h = _modulated(x_ref[...], gpre_ref[...], mod_ref, 1).astype(BF16)
    zm = _dot(h, win_ref[:, _Z_CQ:_Z_END])
    z = _dot(h, win_ref[:, 0:_Z_CQ])
    cq = _rms(zm[:, 0:_Z_CKV - _Z_CQ], qn_ref[...]).astype(BF16)
    ckv = _rms(zm[:, _Z_CKV - _Z_CQ:_Z_KR - _Z_CQ], kvn_ref[...]).astype(BF16)
    q = _dot(cq, wuq_ref[...])
    kv = _dot(ckv, wukv_ref[...])
    nqk = GLA_HEADS * GLA_DK
    gla_ref[:, 0:nqk] = z[:, 0:nqk] * (GLA_DK ** -0.5)
    gla_ref[:, nqk:_Z_DEC] = z[:, nqk:_Z_DEC]
    xd = _dot(z[:, _Z_DEC:_Z_FOUR].astype(BF16), wdec_ref[...]) + bdec_ref[...]
    logsig = jnp.minimum(xd, 0.0) - jnp.log1p(jnp.exp(-jnp.abs(xd)))
    gla_ref[:, _Z_DEC:_Z_DEC + 2 * nqk] = logsig * (1.0 / GLA_TAU)
    fz_ref[...] = z[:, _Z_FOUR:_Z_CQ]
    tab = tab_ref[...]
    nk = MLA_HEADS * MLA_SLOT
    kr = _rope_slot(pltpu.roll(zm[:, _Z_KR - _Z_CQ:], MLA_NOPE, 1), tab)
    scale = (MLA_NOPE + MLA_ROPE) ** -0.5 * LOG2_E
    for hd in range(MLA_HEADS):
        sl = slice(hd * MLA_SLOT, (hd + 1) * MLA_SLOT)
        q_ref[:, sl] = (_rope_slot(q[:, sl], tab) * scale).astype(BF16)
        k_ref[:, sl] = (kv[:, sl] + kr).astype(BF16)
    lane = lax.broadcasted_iota(jnp.int32, (x_ref.shape[0], nk), 1)
    v_ref[...] = jnp.where(lane % MLA_SLOT == MLA_V, 1.0, kv[:, nk:]).astype(BF16)


def _in_projection(x_all, mods, layer, g_pre, w_in2, wdec, bdec, qn, wuq, kvn, wukv, tab,
                   *, tm, cond_of_tile, tab_of_tile, kv_of_tile):
    t, d = x_all.shape
    nq = MLA_HEADS * MLA_SLOT
    row = lambda w: pl.BlockSpec((tm, w), lambda i: (i, 0))
    kv_row = pl.BlockSpec((tm, nq), lambda i: (kv_of_tile(i), 0))
    gla_row = pl.BlockSpec((tm, 2048), lambda i: (kv_of_tile(i), 0))
    return pl.pallas_call(
        _inproj_kernel,
        grid=(t // tm,),
        in_specs=[
            row(d),
            pl.BlockSpec((None, None, N_MOD, d), lambda i: (layer, cond_of_tile(i), 0, 0)),
            _const_spec((1, d)),
            _const_spec(w_in2.shape),
            _const_spec(wdec.shape),
            _const_spec(bdec.shape),
            _const_spec(qn.shape),
            _const_spec(wuq.shape),
            _const_spec(kvn.shape),
            _const_spec(wukv.shape),
            pl.BlockSpec((tm, 2 * LANES), lambda i: (tab_of_tile(i), 0)),
        ],
        out_specs=[gla_row, row(512), row(nq), kv_row, kv_row],
        out_shape=[
            jax.ShapeDtypeStruct((t, 2048), F32),
            jax.ShapeDtypeStruct((t, 512), F32),
            jax.ShapeDtypeStruct((t, nq), BF16),
            jax.ShapeDtypeStruct((t, nq), BF16),
            jax.ShapeDtypeStruct((t, nq), BF16),
        ],
        compiler_params=_params("parallel"),
        name="in_projection",
    )(x_all, mods, g_pre.reshape(1, d), w_in2, wdec, bdec, qn, wuq, kvn, wukv, tab)


def _gla_kernel(*refs, reverse, final):
    if final:
        g_ref, of_ref, gn_ref, o_ref, st_ref = refs
    else:
        g_ref, o_ref, st_ref = refs
    c = GLA_CHUNK
    n_batch, tb = g_ref.shape[0], g_ref.shape[1]
    nqk = GLA_HEADS * GLA_DK
    nv = GLA_HEADS * GLA_DV

    @pl.when(pl.program_id(0) == 0)
    def _():
        st_ref[...] = jnp.zeros(st_ref.shape, F32)

    row = lax.broadcasted_iota(jnp.int32, (c, c), 0)
    col = lax.broadcasted_iota(jnp.int32, (c, c), 1)
    keep = (col >= row) if reverse else (col <= row)
    lmat = jnp.where(keep, 1.0, 0.0).astype(BF16)
    la_off = _Z_DEC + (nqk if reverse else 0)
    mid = c // 2 if reverse else c // 2 - 1
    end = 0 if reverse else c - 1
    chunks = range(tb // c)
    order = list(reversed(chunks) if reverse else chunks)
    work = [(bi, slice(ci * c, (ci + 1) * c)) for bi in range(n_batch) for ci in order]
    heads = [(slice(hd * GLA_DK, (hd + 1) * GLA_DK), slice(hd * GLA_DV, (hd + 1) * GLA_DV))
             for hd in range(GLA_HEADS)]

    cums = []
    for bi, rows in work:
        la = g_ref[bi, rows, la_off:la_off + nqk]
        h1 = la.astype(BF16)
        r1 = la - h1.astype(F32)
        h2 = r1.astype(BF16)
        h3 = (r1 - h2.astype(F32)).astype(BF16)
        cums.append(_dot(lmat, h1) + _dot(lmat, h2) + _dot(lmat, h3))
    scaled = []
    for (bi, rows), b in zip(work, cums):
        ref = b[mid:mid + 1, :]
        tot = b[end:end + 1, :]
        qt = g_ref[bi, rows, 0:nqk] * jnp.exp(b - ref)
        kt = g_ref[bi, rows, nqk:2 * nqk] * jnp.exp(jnp.minimum(ref - b, GLA_EXP_CLAMP))
        q_in = (qt * jnp.exp(ref)).astype(BF16)
        k_st = (kt * jnp.exp(tot - ref)).astype(BF16)
        scaled.append((qt.astype(BF16), kt.astype(BF16), q_in, k_st, jnp.exp(tot)))
    scores = []
    for qt, kt, _, _, _ in scaled:
        scores.append([jnp.where(keep, lax.dot_general(qt[:, ks], kt[:, ks], NT, preferred_element_type=F32), 0.0)
                       .astype(BF16) for ks, _ in heads])
    intra, upd = [], []
    for (bi, rows), a, (_, _, _, k_st, _) in zip(work, scores, scaled):
        vals = [g_ref[bi, rows, 2 * nqk + vs.start:2 * nqk + vs.stop].astype(BF16) for _, vs in heads]
        intra.append([_dot(a[hd], vals[hd]) for hd in range(GLA_HEADS)])
        upd.append([lax.dot_general(vals[hd], k_st[:, heads[hd][0]], TN, preferred_element_type=F32)
                    for hd in range(GLA_HEADS)])
    states = {bi: [st_ref[bi, hd] for hd in range(GLA_HEADS)] for bi in range(n_batch)}
    for (bi, rows), o_in, u, (_, _, q_in, _, dec) in zip(work, intra, upd, scaled):
        for hd, (ks, vs) in enumerate(heads):
            st = states[bi][hd]
            o = o_in[hd] + lax.dot_general(q_in[:, ks], st.astype(BF16), NT, preferred_element_type=F32)
            states[bi][hd] = st * dec[:, ks] + u[hd]
            if final:
                o = o + of_ref[bi, rows, vs]
                gate = g_ref[bi, rows, 2 * nqk + nv + vs.start:2 * nqk + nv + vs.stop]
                o_ref[bi, rows, vs] = (_rms(o, gn_ref[...]) * _silu(gate)).astype(o_ref.dtype)
            else:
                o_ref[bi, rows, vs] = o
    for bi in range(n_batch):
        for hd in range(GLA_HEADS):
            st_ref[bi, hd] = states[bi][hd]


def _gla_scan(gla_in, o_fwd, g_norm, *, reverse, tb, seq, ctx):
    batch = gla_in.shape[0]
    n_lat, n_ctx = seq // tb, ctx // tb

    def blk(j):
        if reverse:
            return jnp.where(j < n_ctx, n_lat + (n_ctx - 1 - j), n_lat + n_ctx - 1 - j)
        return jnp.where(j < n_ctx, n_lat + j, j - n_ctx)

    final = o_fwd is not None
    dv = GLA_HEADS * GLA_DV
    in_specs = [pl.BlockSpec((batch, tb, gla_in.shape[2]), lambda j: (0, blk(j), 0))]
    args = [gla_in]
    if final:
        in_specs += [pl.BlockSpec((batch, tb, dv), lambda j: (0, blk(j), 0)), _const_spec((1, GLA_DV))]
        args += [o_fwd, g_norm.reshape(1, GLA_DV)]
    return pl.pallas_call(
        functools.partial(_gla_kernel, reverse=reverse, final=final),
        grid=(n_lat + n_ctx,),
        in_specs=in_specs,
        out_specs=pl.BlockSpec((batch, tb, dv), lambda j: (0, blk(j), 0)),
        out_shape=jax.ShapeDtypeStruct((batch, seq + ctx, dv), BF16 if final else F32),
        scratch_shapes=[pltpu.VMEM((batch, GLA_HEADS, GLA_DV, GLA_DK), F32)],
        compiler_params=_params("arbitrary"),
        name="gla_bwd_scan" if reverse else "gla_fwd_scan",
    )(*args)


def _dft_mats(n):
    k = np.arange(n)
    ang = (2.0 * np.pi / n) * ((k[:, None] * k[None, :]) % n)
    return np.cos(ang), np.sin(ang)


def _fft_kernel(x_ref, fa_ref, twc_ref, tws_ref, mb_ref, cs_ref, o_ref, xt_ref, zr_ref, zi_ref,
                *, na, nb, bg, kg, pitch):
    ch = x_ref.shape[1]
    fa = fa_ref[...].astype(BF16)
    for a in range(na):
        xt_ref[pl.ds(a, nb, stride=pitch), :] = x_ref[a * nb:(a + 1) * nb, :]
    for g0 in range(0, nb, bg * kg):
        starts = range(g0, min(g0 + bg * kg, nb), bg)
        xs = [jnp.concatenate([xt_ref[(b0 + i) * pitch:(b0 + i) * pitch + na, :] for i in range(bg)],
                              axis=1).astype(BF16) for b0 in starts]
        zs = [_dot(fa, x) for x in xs]
        for b0, z in zip(starts, zs):
            for i in range(bg):
                r0 = (b0 + i) * pitch
                zr_ref[r0:r0 + na, :] = z[:na, i * ch:(i + 1) * ch]
                zi_ref[r0:r0 + na, :] = z[na:, i * ch:(i + 1) * ch]
    mb = mb_ref[...].astype(BF16)
    cs = cs_ref[...].astype(BF16)

    for k0 in range(0, na, kg):
        group = range(k0, min(k0 + kg, na))
        ts = []
        for ka in group:
            zr, zi = zr_ref[pl.ds(ka, nb, stride=pitch), :], zi_ref[pl.ds(ka, nb, stride=pitch), :]
            c = twc_ref[ka * nb:(ka + 1) * nb, :]
            s = tws_ref[ka * nb:(ka + 1) * nb, :]
            ts.append(jnp.concatenate([zr * c + zi * s, zi * c - zr * s], axis=0).astype(BF16))
        vs = [_dot(mb, t) for t in ts]
        vvs = [jnp.concatenate([v[:nb], v[nb:]], axis=1).astype(BF16) for v in vs]
        outs = [_dot(vv, cs) for vv in vvs]
        for ka, out in zip(group, outs):
            o_ref[pl.ds(ka, nb, stride=na), :] = out


def _fourier_latent(fz, *, batch, seq):
    cw = fz.shape[1]
    na, nb = FFT_NA, seq // FFT_NA
    pitch = na + 1
    ch = FOURIER_CH
    ca, sa = _dft_mats(na)
    f_a = np.concatenate([ca, -sa], axis=0).astype(np.float32)
    cb, sb = _dft_mats(nb)
    m_b = np.concatenate([np.concatenate([cb, sb], axis=1),
                          np.concatenate([-sb, cb], axis=1)], axis=0).astype(np.float32)
    cc, sc = _dft_mats(ch)
    cs = (np.concatenate([cc, sc], axis=0) * ((seq * ch) ** -0.5)).astype(np.float32)
    ang = (2.0 * np.pi / seq) * ((np.arange(na)[:, None] * np.arange(nb)[None, :]) % seq)
    twc = jnp.broadcast_to(np.cos(ang).reshape(seq, 1).astype(np.float32), (seq, ch))
    tws = jnp.broadcast_to(np.sin(ang).reshape(seq, 1).astype(np.float32), (seq, ch))
    return pl.pallas_call(
        functools.partial(_fft_kernel, na=na, nb=nb, bg=4, kg=16, pitch=pitch),
        grid=(batch, cw // ch),
        in_specs=[pl.BlockSpec((seq, ch), lambda b, g: (b, g)), _const_spec(f_a.shape),
                  _const_spec((seq, ch)), _const_spec((seq, ch)), _const_spec(m_b.shape), _const_spec(cs.shape)],
        out_specs=pl.BlockSpec((seq, ch), lambda b, g: (b, g)),
        out_shape=jax.ShapeDtypeStruct((batch * seq, cw), F32),
        scratch_shapes=[pltpu.VMEM((nb * pitch, ch), F32)] * 3,
        compiler_params=_params("parallel", "parallel"),
        name="fnet_latent",
    )(fz, f_a, twc, tws, m_b, cs)


def _fft_ctx_kernel(x_ref, f_ref, cs_ref, o_ref):
    ch = FOURIER_CH
    f = f_ref[...].astype(BF16)
    cs = cs_ref[...].astype(BF16)
    for g in range(x_ref.shape[1] // ch):
        ab = _dot(x_ref[:, g * ch:(g + 1) * ch].astype(BF16), cs)
        st = jnp.concatenate([ab[:, :ch], ab[:, ch:]], axis=0).astype(BF16)
        o_ref[:, g * ch:(g + 1) * ch] = _dot(f, st)


def _fourier_context(fz, *, batch, seq, ctx):
    cw = fz.shape[1]
    ch = FOURIER_CH
    cn, sn = _dft_mats(ctx)
    f = (np.concatenate([cn, -sn], axis=1) * ((ctx * ch) ** -0.5)).astype(np.float32)
    cc, sc = _dft_mats(ch)
    cs = np.concatenate([cc, sc], axis=1).astype(np.float32)
    blk0 = batch * seq // ctx
    return pl.pallas_call(
        _fft_ctx_kernel,
        grid=(batch,),
        in_specs=[
            pl.BlockSpec((ctx, cw), lambda b: (blk0 + b, 0)),
            _const_spec(f.shape),
            _const_spec(cs.shape),
        ],
        out_specs=pl.BlockSpec((ctx, cw), lambda b: (b, 0)),
        out_shape=jax.ShapeDtypeStruct((batch * ctx, cw), F32),
        compiler_params=_params("parallel"),
        name="fnet_context",
    )(fz, f, cs)


def _attn_update(s, v, carry):
    m, acc = carry
    m_new = jnp.maximum(m, jnp.max(s, axis=-1, keepdims=True))
    p = jnp.exp2(s - m_new).astype(BF16)
    acc = jnp.exp2(m - m_new) * acc + _dot(p, v)
    return m_new, acc


def _attn_kernel(q_ref, k_ref, v_ref, o_ref, *, kc):
    tq = q_ref.shape[0]
    sl = [slice(0, MLA_SLOT), slice(MLA_SLOT, 2 * MLA_SLOT)]
    qs = [q_ref[:, s] for s in sl]
    init = (jnp.full((tq, 1), -jnp.inf, F32), jnp.zeros((tq, MLA_SLOT), F32))
    carry = (init, init)
    bounds = [(c0, c0 + kc) for c0 in range(0, k_ref.shape[0], kc)]

    def scores(lo, hi):
        return [lax.dot_general(qs[h], k_ref[lo:hi, sl[h]], NT, preferred_element_type=F32) for h in range(2)]

    s_next = scores(*bounds[0])
    for i, (lo, hi) in enumerate(bounds):
        s_cur = s_next
        if i + 1 < len(bounds):
            s_next = scores(*bounds[i + 1])
        carry = tuple(_attn_update(s_cur[h], v_ref[lo:hi, sl[h]], carry[h]) for h in range(2))
    outs = [acc * (1.0 / acc[:, MLA_V:MLA_V + 1]) for (_, acc) in carry]
    lane = lax.broadcasted_iota(jnp.int32, outs[0].shape, 1)
    o_ref[...] = jnp.where(lane < MLA_V, outs[0], pltpu.roll(outs[1], MLA_V, 1)).astype(o_ref.dtype)


def _attention(q, k, v, *, latent, batch, seq, ctx, tq, kc):
    hp = MLA_HEADS // 2
    w = 2 * MLA_SLOT
    wv = 2 * MLA_V
    if latent:
        nq = seq // tq
        q_spec = pl.BlockSpec((tq, w), lambda b, h, i: (b * nq + i, h))
        kv_spec = pl.BlockSpec((seq + ctx, w), lambda b, h, i: (b, h))
    else:
        nq = ctx // tq
        q0 = batch * seq // tq
        per_batch = (seq + ctx) // ctx
        q_spec = pl.BlockSpec((tq, w), lambda b, h, i: (q0 + b * nq + i, h))
        kv_spec = pl.BlockSpec((ctx, w), lambda b, h, i: (b * per_batch + seq // ctx, h))
    return pl.pallas_call(
        functools.partial(_attn_kernel, kc=kc),
        grid=(batch, hp, nq),
        in_specs=[q_spec, kv_spec, kv_spec],
        out_specs=pl.BlockSpec((tq, wv), lambda b, h, i: (b * nq + i, h)),
        out_shape=jax.ShapeDtypeStruct((batch * nq * tq, hp * wv), BF16),
        compiler_params=_params("parallel", "parallel", "arbitrary"),
        name="mla_latent" if latent else "mla_context",
    )(q, k, v)


def _merge_kernel(*refs, n_lat_tiles, with_ctx, n_sub):
    x_ref, mod_ref, gpre_ref, gpost_ref, wgt_ref = refs[:5]
    ya_refs = refs[5:5 + n_sub]
    rest = refs[5 + n_sub:]
    if with_ctx:
        yb_ref, yc_ref, ybc_ref, ycc_ref, wb_ref, wo_ref, o_ref = rest
        is_ctx = pl.program_id(0) >= n_lat_tiles
    else:
        yb_ref, yc_ref, wb_ref, wo_ref, o_ref = rest
    d = x_ref.shape[1]
    sub = x_ref.shape[0] // n_sub
    xs, branches = [], []
    for j in range(n_sub):
        rows = slice(j * sub, (j + 1) * sub)
        yb, yc = yb_ref[rows, :], yc_ref[rows, :]
        if with_ctx:
            yb = jnp.where(is_ctx, ybc_ref[rows, :], yb)
            yc = jnp.where(is_ctx, ycc_ref[rows, :], yc)
        ys = (ya_refs[j][...], yb.astype(BF16), yc)
        branches.append([_dot(ys[br], wb_ref[br]) for br in range(3)])
        xs.append(x_ref[rows, :])
    hs = [_modulated(x, gpre_ref[...], mod_ref, 1).astype(BF16) for x in xs]
    gates = [[_dot(h, wgt_ref[:, br * d:(br + 1) * d]) for br in range(3)] for h in hs]
    ms = []
    for j in range(n_sub):
        m = jnp.zeros(xs[j].shape, F32)
        for br in range(3):
            m = m + jax.nn.sigmoid(gates[j][br]) * branches[j][br]
        ms.append(m.astype(BF16))
    outs = [_dot(m, wo_ref[...]) for m in ms]
    for j in range(n_sub):
        o_ref[j * sub:(j + 1) * sub, :] = xs[j] + mod_ref[5:6, :] * _rms(outs[j], gpost_ref[...])


def _merge(x_all, mods, layer, g_pre, g_post, w_gates, ya, yb, yc, yb_ctx, yc_ctx, wb, wo,
           *, n_rows, n_lat, tm, n_sub, cond_of_tile, kv_of_tile):
    d = x_all.shape[1]
    bw = ya.shape[1]
    with_ctx = yb_ctx is not None
    big = tm * n_sub
    n_lat_tiles = n_lat // big
    row = lambda w: pl.BlockSpec((big, w), lambda i: (i, 0))
    lat = lambda w: pl.BlockSpec((big, w), lambda i: (jnp.minimum(i, n_lat_tiles - 1), 0))
    ctx = lambda w: pl.BlockSpec((big, w), lambda i: (jnp.maximum(i - n_lat_tiles, 0), 0))
    ya_specs = [pl.BlockSpec((tm, bw), lambda i, j=j: (kv_of_tile(i * n_sub + j), 0)) for j in range(n_sub)]
    branch_specs = ya_specs + [lat(bw), lat(bw)] + ([ctx(bw), ctx(bw)] if with_ctx else [])
    branch_args = [ya] * n_sub + [yb, yc] + ([yb_ctx, yc_ctx] if with_ctx else [])
    return pl.pallas_call(
        functools.partial(_merge_kernel, n_lat_tiles=n_lat_tiles, with_ctx=with_ctx, n_sub=n_sub),
        grid=(n_rows // big,),
        in_specs=[
            row(d),
            pl.BlockSpec((None, None, N_MOD, d), lambda i: (layer, cond_of_tile(i * n_sub), 0, 0)),
            _const_spec((1, d)),
            _const_spec((1, d)),
            _const_spec(w_gates.shape),
            *branch_specs,
            pl.BlockSpec((None,) + wb.shape[1:], lambda i: (layer, 0, 0, 0), pipeline_mode=pl.Buffered(1)),
            pl.BlockSpec((None,) + wo.shape[1:], lambda i: (layer, 0, 0), pipeline_mode=pl.Buffered(1)),
        ],
        out_specs=row(d),
        out_shape=jax.ShapeDtypeStruct((n_rows, d), F32),
        compiler_params=_params("parallel"),
        name="merge",
    )(x_all, mods, g_pre.reshape(1, d), g_post.reshape(1, d), w_gates, *branch_args, wb, wo)


def _rope_table(seq, tm):
    pos = np.arange(seq)
    half = MLA_ROPE // 2
    inv_freq = ROPE_BASE ** (-np.arange(0, half, 2) / half)
    ar, ac = (pos // GRID_W)[:, None] * inv_freq, (pos % GRID_W)[:, None] * inv_freq
    cos = np.concatenate([np.cos(ar), np.cos(ar), np.cos(ac), np.cos(ac)], axis=1)
    sin = np.concatenate([-np.sin(ar), np.sin(ar), -np.sin(ac), np.sin(ac)], axis=1)
    pad = LANES - MLA_NOPE - MLA_ROPE
    rows = lambda n, c, s: np.concatenate(
        [np.ones((n, MLA_NOPE)), c, np.zeros((n, pad)), np.zeros((n, MLA_NOPE)), s, np.zeros((n, pad))], axis=1)
    tab = rows(seq, cos, sin)
    ident = rows(tm, np.ones((tm, MLA_ROPE)), np.zeros((tm, MLA_ROPE)))
    return np.concatenate([tab, ident], axis=0).astype(np.float32)


def _with_rope_partners(w):
    j = np.arange(MLA_ROPE)
    h = MLA_ROPE // 4
    partner = np.where((j % (2 * h)) < h, j + h, j - h)
    rope = w[..., -MLA_ROPE:]
    return jnp.concatenate([w, rope[..., partner]], axis=-1)


def _pad_cols(w, n):
    return jnp.pad(w, ((0, 0), (0, n - w.shape[1])))


def _layer_weights(layer, w_in, gla_w_decay, gla_b_decay, mla_w_uq, mla_w_ukv):
    nqk = GLA_HEADS * GLA_DK
    r = GLA_GATE_RANK
    o_dec = 2 * nqk + 2 * GLA_HEADS * GLA_DV
    o_four = o_dec + 2 * r
    o_cq = o_four + FOURIER_GROUPS * FOURIER_CH
    q_rank = mla_w_uq.shape[1]
    kv_rank = mla_w_ukv.shape[1]
    o_ckv = o_cq + q_rank
    o_kr = o_ckv + kv_rank
    o_gates = o_kr + MLA_ROPE
    w = w_in[layer]
    w_in2 = jnp.concatenate([
        w[:, :o_dec], _pad_cols(w[:, o_dec:o_four], LANES), w[:, o_four:o_cq], w[:, o_cq:o_ckv],
        w[:, o_ckv:o_kr], _pad_cols(_with_rope_partners(w[:, o_kr:o_gates]), LANES)], axis=1).astype(BF16)
    w_gates = w[:, o_gates:].astype(BF16)
    wd = gla_w_decay[layer]
    wdec = jnp.zeros((LANES, 2 * nqk), F32)
    wdec = wdec.at[0:r, 0:nqk].set(wd[0]).at[r:2 * r, nqk:].set(wd[1]).astype(BF16)
    bdec = gla_b_decay[layer].reshape(1, 2 * nqk)
    dq = MLA_NOPE + MLA_ROPE
    wuq = _with_rope_partners(mla_w_uq[layer].reshape(q_rank, MLA_HEADS, dq))
    wuq = wuq.reshape(q_rank, MLA_HEADS * MLA_SLOT).astype(BF16)
    wkv = mla_w_ukv[layer].reshape(kv_rank, MLA_HEADS, MLA_NOPE + MLA_V)
    wk = jnp.pad(wkv[:, :, :MLA_NOPE], ((0, 0), (0, 0), (0, MLA_SLOT - MLA_NOPE)))
    wv = jnp.pad(wkv[:, :, MLA_NOPE:], ((0, 0), (0, 0), (0, MLA_SLOT - MLA_V)))
    wukv = jnp.concatenate([wk.reshape(kv_rank, MLA_HEADS * MLA_SLOT),
                            wv.reshape(kv_rank, MLA_HEADS * MLA_SLOT)], axis=1).astype(BF16)
    return w_in2, w_gates, wdec, bdec, wuq, wukv


def kernel(x, c, ctx, c_ctx, w_mod, b_mod, norm_pre, norm_post, ffn_w_gate, ffn_w_up, ffn_w_down, w_in,
           gla_w_decay, gla_b_decay, gla_norm, mla_q_norm, mla_w_uq, mla_kv_norm, mla_w_ukv, w_branch, w_out):
    batch, seq, d = x.shape
    n_ctx = ctx.shape[1]
    depth = w_mod.shape[0]
    n_lat = batch * seq
    t = n_lat + batch * n_ctx
    tm = min(256, n_ctx)
    tm_ffn = FFN_TM if seq % FFN_TM == 0 and (batch * n_ctx) % FFN_TM == 0 else tm
    d_ff = ffn_w_gate.shape[-1]
    fc = 256 if d_ff % 256 == 0 else d_ff
    assert seq % tm == 0 and n_ctx % tm == 0 and seq % (FFT_NA * 8) == 0 and batch < MOD_ROWS

    cond_of_tile = lambda i: jnp.minimum((i * tm) // seq, batch)
    tab_of_tile = lambda i: jnp.where(i * tm < n_lat, ((i * tm) % seq) // tm, seq // tm)
    lat_tiles, ctx_tiles = seq // tm, n_ctx // tm

    def kv_of_tile(i):
        j = i - batch * lat_tiles
        lat = (i // lat_tiles) * (lat_tiles + ctx_tiles) + i % lat_tiles
        ctx_row = (j // ctx_tiles) * (lat_tiles + ctx_tiles) + lat_tiles + j % ctx_tiles
        return jnp.where(j < 0, lat, ctx_row)

    n_keys = seq + n_ctx
    key_chunk = max(kc for kc in range(LANES, min(ATTN_KC, n_keys) + 1, LANES) if n_keys % kc == 0)

    cond = jnp.concatenate([c, c_ctx[None, :], jnp.zeros((MOD_ROWS - batch - 1, d), F32)], axis=0)
    mods = _modulation(cond, w_mod, b_mod, tk=LANES).reshape(depth, MOD_ROWS, N_MOD, d)
    tab = _rope_table(seq, tm)
    x_all, x_ctx = x.reshape(n_lat, d), ctx.reshape(batch * n_ctx, d)
    wg_all, wu_all, wd_all = ffn_w_gate, ffn_w_up, ffn_w_down
    wb_all, wo_all = w_branch.astype(BF16), w_out.astype(BF16)

    for layer in range(depth):
        last = layer == depth - 1
        ffn = lambda xa, xc, sub, s, rows: _half_ffn(
            xa, xc, mods, layer, sub, s, norm_pre[layer, sub], norm_post[layer, sub], wg_all, wu_all, wd_all,
            n_rows=rows, tm=tm_ffn, cond_of_tile=lambda i: jnp.minimum((i * tm_ffn) // seq, batch), fc=fc)
        x_all = ffn(x_all, x_ctx if layer == 0 else None, 0, 0, t)

        w_in2, w_gates, wdec, bdec, wuq, wukv = _layer_weights(
            layer, w_in, gla_w_decay, gla_b_decay, mla_w_uq, mla_w_ukv)
        gla_in, fz, q, k, v = _in_projection(
            x_all, mods, layer, norm_pre[layer, 1], w_in2, wdec, bdec,
            mla_q_norm[layer].reshape(1, -1), wuq, mla_kv_norm[layer].reshape(1, -1), wukv, tab,
            tm=tm, cond_of_tile=cond_of_tile, tab_of_tile=tab_of_tile, kv_of_tile=kv_of_tile)

        scan = functools.partial(_gla_scan, tb=tm, seq=seq, ctx=n_ctx)
        gla_in = gla_in.reshape(batch, n_keys, gla_in.shape[1])
        o_fwd = scan(gla_in, None, None, reverse=False)
        ya = scan(gla_in, o_fwd, gla_norm[layer], reverse=True).reshape(batch * n_keys, -1)

        rows_out = n_lat if last else t
        yb = _fourier_latent(fz, batch=batch, seq=seq)
        yc = _attention(q, k, v, latent=True, batch=batch, seq=seq, ctx=n_ctx,
                        tq=min(ATTN_TQ, seq), kc=key_chunk)
        yb_ctx = yc_ctx = None
        if not last:
            yb_ctx = _fourier_context(fz, batch=batch, seq=seq, ctx=n_ctx)
            yc_ctx = _attention(q, k, v, latent=False, batch=batch, seq=seq, ctx=n_ctx, tq=tm, kc=n_ctx)

        x_all = _merge(x_all, mods, layer, norm_pre[layer, 1], norm_post[layer, 1], w_gates, ya, yb, yc,
                       yb_ctx, yc_ctx, wb_all, wo_all,
                       n_rows=rows_out, n_lat=n_lat, tm=tm, n_sub=tm_ffn // tm, cond_of_tile=cond_of_tile,
                       kv_of_tile=kv_of_tile)
        x_all = ffn(x_all, None, 2, 1, rows_out)
    return x_all.reshape(batch, seq, d)
```
